```python
import math
import jax, jax.numpy as jnp
from jax import lax
import numpy as np

D_MODEL = 1024
BATCH = 8
SEQ = 4096
DEPTH = 1

HEAD_DIM = 64
D_MIX = D_MODEL
D_MIX_A = D_MIX // 2
D_MIX_B = D_MIX - D_MIX_A
N_HEADS_A = D_MIX_A // HEAD_DIM
DIFF_QK_DIM = HEAD_DIM // 2
N_HEADS_B = D_MIX_B // HEAD_DIM
GQA_GROUP = 4
N_KV_B = N_HEADS_B // GQA_GROUP
D_FF = ((8 * D_MODEL // 3 + 127) // 128) * 128
GRID_W = 64
ROPE_THETA = 10000.0
ROPE_AXIS_DIM = HEAD_DIM // 2
N_BUCKETS = 32
MAX_DISTANCE = 128
Q_BLOCK = 128
EPS = 1e-6
COLS_QA = N_HEADS_A * 2 * DIFF_QK_DIM
COLS_KA = N_HEADS_A * 2 * DIFF_QK_DIM
COLS_VA = N_HEADS_A * HEAD_DIM
COLS_QB = N_HEADS_B * HEAD_DIM
COLS_KB = N_KV_B * HEAD_DIM
COLS_VB = N_KV_B * HEAD_DIM
D_IN_PROJ = COLS_QA + COLS_KA + COLS_VA + COLS_QB + COLS_KB + COLS_VB

kernel_name = "hybrid_diffattn_gqa_axialrope_macaron"


def rms_norm(x, g):
    xf = x.astype(jnp.float32)
    y = xf * lax.rsqrt(jnp.mean(xf * xf, axis=-1, keepdims=True) + EPS)
    return (y * g.astype(jnp.float32)).astype(x.dtype)


def swiglu(x, w_gate, w_up, w_down):
    return (jax.nn.silu(x @ w_gate) * (x @ w_up)) @ w_down


def t5_bucket(rel):
    nb = N_BUCKETS // 2
    max_exact = nb // 2
    ret = jnp.where(rel > 0, nb, 0)
    n = jnp.abs(rel)
    nf = jnp.maximum(n, 1).astype(jnp.float32)
    large = max_exact + (jnp.log(nf / max_exact) / math.log(MAX_DISTANCE / max_exact)
                         * (nb - max_exact)).astype(jnp.int32)
    large = jnp.minimum(large, nb - 1)
    return ret + jnp.where(n < max_exact, n, large)


def axial_rope_tables(S):
    rows = S // GRID_W
    row = jnp.repeat(jnp.arange(rows, dtype=jnp.int32), GRID_W).astype(jnp.float32)
    col = jnp.tile(jnp.arange(GRID_W, dtype=jnp.int32), rows).astype(jnp.float32)
    inv = ROPE_THETA ** (-jnp.arange(0, ROPE_AXIS_DIM, 2, dtype=jnp.float32) / ROPE_AXIS_DIM)
    ang = jnp.concatenate([row[:, None] * inv[None], col[:, None] * inv[None]], axis=-1)
    return jnp.cos(ang), jnp.sin(ang)


def apply_rope(x, cos, sin):
    xf = x.astype(jnp.float32).reshape(x.shape[:-1] + (HEAD_DIM // 2, 2))
    x0, x1 = xf[..., 0], xf[..., 1]
    out = jnp.stack([x0 * cos - x1 * sin, x0 * sin + x1 * cos], axis=-1)
    return out.reshape(x.shape).astype(x.dtype)


def split_blocks(t):
    B, H, S, d = t.shape
    return t.reshape(B, H, S // Q_BLOCK, Q_BLOCK, d).transpose(2, 0, 1, 3, 4)


def merge_blocks(t):
    nblk, B, H, qb, d = t.shape
    return t.transpose(1, 2, 0, 3, 4).reshape(B, H, nblk * qb, d)


def diff_attention(q1, q2, k1, k2, v, lam, rel_bias):
    S = q1.shape[2]
    nblk = S // Q_BLOCK
    scale = DIFF_QK_DIM ** -0.5
    kpos = jnp.arange(S, dtype=jnp.int32)
    table = rel_bias.astype(jnp.float32)

    def block(args):
        i, a1, a2 = args
        qpos = i * Q_BLOCK + jnp.arange(Q_BLOCK, dtype=jnp.int32)
        bias = table[t5_bucket(kpos[None, :] - qpos[:, None])]
        bias = bias.transpose(2, 0, 1)[None]
        s1 = jnp.einsum('bhqd,bhkd->bhqk', a1, k1, preferred_element_type=jnp.float32) * scale + bias
        s2 = jnp.einsum('bhqd,bhkd->bhqk', a2, k2, preferred_element_type=jnp.float32) * scale + bias
        p = jax.nn.softmax(s1, axis=-1) - lam * jax.nn.softmax(s2, axis=-1)
        return jnp.einsum('bhqk,bhkd->bhqd', p.astype(v.dtype), v)

    out = lax.map(block, (jnp.arange(nblk, dtype=jnp.int32), split_blocks(q1), split_blocks(q2)))
    return merge_blocks(out)


def gqa_attention(q, k, v):
    B, HB, S, d = q.shape
    scale = d ** -0.5

    def block(qb):
        qg = qb.reshape(B, N_KV_B, GQA_GROUP, Q_BLOCK, d)
        s = jnp.einsum('bkgqd,bksd->bkgqs', qg, k, preferred_element_type=jnp.float32) * scale
        p = jax.nn.softmax(s, axis=-1)
        o = jnp.einsum('bkgqs,bksd->bkgqd', p.astype(v.dtype), v)
        return o.reshape(B, HB, Q_BLOCK, d)

    return merge_blocks(lax.map(block, split_blocks(q)))


def setup_inputs(seed: int = 0) -> dict:
    key = jax.random.key(seed)
    ks = jax.random.split(key, 24)
    f32 = jnp.float32

    def w(k, shape, fan_in):
        return jax.random.normal(k, shape, f32) * fan_in ** -0.5

    def gain(k, shape):
        return 1.0 + 0.01 * jax.random.normal(k, shape, f32)

    L = DEPTH
    return {
        "x": jax.random.normal(ks[0], (BATCH, SEQ, D_MODEL), f32),
        "ffn1_norm": gain(ks[1], (L, D_MODEL)),
        "ffn1_w_gate": w(ks[2], (L, D_MODEL, D_FF), D_MODEL),
        "ffn1_w_up": w(ks[3], (L, D_MODEL, D_FF), D_MODEL),
        "ffn1_w_down": w(ks[4], (L, D_FF, D_MODEL), D_FF),
        "mix_norm": gain(ks[5], (L, D_MODEL)),
        "w_in": w(ks[6], (L, D_MODEL, D_IN_PROJ), D_MODEL),
        "lambda_q1": 0.1 * jax.random.normal(ks[7], (L, DIFF_QK_DIM), f32),
        "lambda_k1": 0.1 * jax.random.normal(ks[8], (L, DIFF_QK_DIM), f32),
        "lambda_q2": 0.1 * jax.random.normal(ks[9], (L, DIFF_QK_DIM), f32),
        "lambda_k2": 0.1 * jax.random.normal(ks[10], (L, DIFF_QK_DIM), f32),
        "diff_subln": gain(ks[11], (L, HEAD_DIM)),
        "q_norm": gain(ks[12], (L, HEAD_DIM)),
        "k_norm": gain(ks[13], (L, HEAD_DIM)),
        "rel_bias": 0.5 * jax.random.normal(ks[14], (N_BUCKETS, N_HEADS_A), f32),
        "w_out": w(ks[15], (L, D_MIX, D_MODEL), D_MIX),
        "ffn2_norm": gain(ks[16], (L, D_MODEL)),
        "ffn2_w_gate": w(ks[17], (L, D_MODEL, D_FF), D_MODEL),
        "ffn2_w_up": w(ks[18], (L, D_MODEL, D_FF), D_MODEL),
        "ffn2_w_down": w(ks[19], (L, D_FF, D_MODEL), D_FF),
        "final_norm": gain(ks[20], (D_MODEL,)),
    }


def reference(x, ffn1_norm, ffn1_w_gate, ffn1_w_up, ffn1_w_down, mix_norm, w_in,
              lambda_q1, lambda_k1, lambda_q2, lambda_k2, diff_subln, q_norm, k_norm,
              rel_bias, w_out, ffn2_norm, ffn2_w_gate, ffn2_w_up, ffn2_w_down, final_norm):
    B, S, _ = x.shape
    cos, sin = axial_rope_tables(S)
    offs = np.cumsum([0, COLS_QA, COLS_KA, COLS_VA, COLS_QB, COLS_KB, COLS_VB])

    for l in range(DEPTH):
        lambda_init = 0.8 - 0.6 * math.exp(-0.3 * l)

        x = x + 0.5 * swiglu(rms_norm(x, ffn1_norm[l]), ffn1_w_gate[l], ffn1_w_up[l], ffn1_w_down[l])

        h = rms_norm(x, mix_norm[l])
        proj = h @ w_in[l]
        qa, ka, va, qb, kb, vb = [proj[..., offs[i]:offs[i + 1]] for i in range(6)]

        qa = qa.reshape(B, S, N_HEADS_A, 2, DIFF_QK_DIM).transpose(0, 2, 3, 1, 4)
        ka = ka.reshape(B, S, N_HEADS_A, 2, DIFF_QK_DIM).transpose(0, 2, 3, 1, 4)
        va = va.reshape(B, S, N_HEADS_A, HEAD_DIM).transpose(0, 2, 1, 3)
        lam = (jnp.exp(jnp.sum(lambda_q1[l].astype(jnp.float32) * lambda_k1[l].astype(jnp.float32)))
               - jnp.exp(jnp.sum(lambda_q2[l].astype(jnp.float32) * lambda_k2[l].astype(jnp.float32)))
               + lambda_init)
        oa = diff_attention(qa[:, :, 0], qa[:, :, 1], ka[:, :, 0], ka[:, :, 1], va, lam, rel_bias)
        oa = rms_norm(oa, diff_subln[l]) * (1.0 - lambda_init)
        oa = oa.transpose(0, 2, 1, 3).reshape(B, S, D_MIX_A)

        qb = rms_norm(qb.reshape(B, S, N_HEADS_B, HEAD_DIM), q_norm[l]).transpose(0, 2, 1, 3)
        kb = rms_norm(kb.reshape(B, S, N_KV_B, HEAD_DIM), k_norm[l]).transpose(0, 2, 1, 3)
        vb = vb.reshape(B, S, N_KV_B, HEAD_DIM).transpose(0, 2, 1, 3)
        qb = apply_rope(qb, cos, sin)
        kb = apply_rope(kb, cos, sin)
        ob = gqa_attention(qb, kb, vb).transpose(0, 2, 1, 3).reshape(B, S, D_MIX_B)

        x = x + jnp.concatenate([oa, ob], axis=-1) @ w_out[l]

        x = x + 0.5 * swiglu(rms_norm(x, ffn2_norm[l]), ffn2_w_gate[l], ffn2_w_up[l], ffn2_w_down[l])

    return rms_norm(x, final_norm)
```

```python
import functools
import math

import numpy as np
import jax
import jax.numpy as jnp
from jax import lax
from jax.experimental import pallas as pl
from jax.experimental.pallas import tpu as pltpu

D_MODEL = 1024
HEAD_DIM = 64
D_MIX_A = 512
D_MIX_B = 512
N_HEADS_A = 8
DIFF_QK_DIM = 32
N_HEADS_B = 8
GQA_GROUP = 4
N_KV_B = 2
D_FF = 2816
GRID_W = 64
ROPE_THETA = 10000.0
ROPE_AXIS_DIM = 32
N_BUCKETS = 32
MAX_DISTANCE = 128
EPS = 1e-6
D_IN_PROJ = 2304
LOG2E = math.log2(math.e)

VMEM_LIMIT_BYTES = 56 * 1024 * 1024

FFN_TOKEN_TILE = 512
FFN_CHUNKS = ((0, 1024), (1024, 2048), (2048, 2816))
PROJ_TOKEN_TILE = 512
ATTN_Q_TILE = 512
ATTN_K_TILE = 256

BF16 = jnp.bfloat16
F32 = jnp.float32


def _dot(a, b):
    return jnp.dot(a, b, preferred_element_type=F32)


def _rms_rows(x, g):
    ms = jnp.mean(x * x, axis=-1, keepdims=True)
    return x * lax.rsqrt(ms + EPS) * g


def _swiglu_acc(h, wg_ref, wu_ref, wd_ref):
    y = None
    for s, e in FFN_CHUNKS:
        g = _dot(h, wg_ref[:, s:e])
        u = _dot(h, wu_ref[:, s:e])
        a = (g / (1.0 + jnp.exp(-g)) * u).astype(BF16)
        part = _dot(a, wd_ref[s:e, :])
        y = part if y is None else y + part
    return y


def _ffn_kernel(x_ref, g_ref, wg_ref, wu_ref, wd_ref, o_ref):
    x = x_ref[...]
    h = _rms_rows(x, g_ref[...]).astype(BF16)
    o_ref[...] = x + 0.5 * _swiglu_acc(h, wg_ref, wu_ref, wd_ref)


def _resident(shape):
    return pl.BlockSpec(shape, lambda *_: (0,) * len(shape), pipeline_mode=pl.Buffered(1))


def _ffn(x2d, g, wg, wu, wd):
    n = x2d.shape[0]
    tm = FFN_TOKEN_TILE
    return pl.pallas_call(
        _ffn_kernel,
        grid=(n // tm,),
        in_specs=[
            pl.BlockSpec((tm, D_MODEL), lambda i: (i, 0)),
            _resident((1, D_MODEL)),
            _resident((D_MODEL, D_FF)),
            _resident((D_MODEL, D_FF)),
            _resident((D_FF, D_MODEL)),
        ],
        out_specs=pl.BlockSpec((tm, D_MODEL), lambda i: (i, 0)),
        out_shape=jax.ShapeDtypeStruct((n, D_MODEL), F32),
        compiler_params=pltpu.CompilerParams(
            dimension_semantics=("arbitrary",), vmem_limit_bytes=VMEM_LIMIT_BYTES),
        name="ffn1",
    )(x2d, g, wg, wu, wd)


def _proj_kernel(x_ref, g_ref, wt_ref, cos_ref, sin_ref, gq_ref, gk_ref, o_ref):
    h = _rms_rows(x_ref[...], g_ref[...]).astype(BF16)

    def section(r0, r1):
        return lax.dot_general(wt_ref[r0:r1, :], h, (((1,), (1,)), ((), ())),
                               preferred_element_type=F32)

    o_ref[0:512, :] = (section(0, 512) * (DIFF_QK_DIM ** -0.5 * LOG2E)).astype(BF16)
    o_ref[512:1536, :] = section(512, 1536).astype(BF16)

    cos = cos_ref[...]
    sin = sin_ref[...]

    def norm_rope(t, gain, scale):
        ms = jnp.mean(t * t, axis=0, keepdims=True)
        t = t * lax.rsqrt(ms + EPS) * gain
        te, to = t[:32], t[32:]
        re = te * cos - to * sin
        ro = te * sin + to * cos
        return jnp.concatenate([re, ro], axis=0) * scale

    qb = section(1536, 2048)
    for hh in range(N_HEADS_B):
        r0 = hh * HEAD_DIM
        o_ref[1536 + r0:1536 + r0 + HEAD_DIM, :] = norm_rope(
            qb[r0:r0 + HEAD_DIM], gq_ref[...], HEAD_DIM ** -0.5 * LOG2E).astype(BF16)
    kvb = section(2048, 2304)
    for hh in range(N_KV_B):
        r0 = hh * HEAD_DIM
        o_ref[2048 + r0:2048 + r0 + HEAD_DIM, :] = norm_rope(
            kvb[r0:r0 + HEAD_DIM], gk_ref[...], 1.0).astype(BF16)
    o_ref[2176:2304, :] = kvb[128:256].astype(BF16)


def _proj(x1, g, w_in_t, cos_t, sin_t, gq, gk):
    b, s, _ = x1.shape
    tm = PROJ_TOKEN_TILE
    return pl.pallas_call(
        _proj_kernel,
        grid=(b, s // tm),
        in_specs=[
            pl.BlockSpec((None, tm, D_MODEL), lambda bi, si: (bi, si, 0)),
            _resident((1, D_MODEL)),
            _resident((D_IN_PROJ, D_MODEL)),
            pl.BlockSpec((ROPE_AXIS_DIM, tm), lambda bi, si: (0, si)),
            pl.BlockSpec((ROPE_AXIS_DIM, tm), lambda bi, si: (0, si)),
            _resident((HEAD_DIM, 1)),
            _resident((HEAD_DIM, 1)),
        ],
        out_specs=pl.BlockSpec((None, D_IN_PROJ, tm), lambda bi, si: (bi, 0, si)),
        out_shape=jax.ShapeDtypeStruct((b, D_IN_PROJ, s), BF16),
        compiler_params=pltpu.CompilerParams(
            dimension_semantics=("arbitrary", "arbitrary"), vmem_limit_bytes=VMEM_LIMIT_BYTES),
        name="in_proj",
    )(x1, g, w_in_t, cos_t, sin_t, gq, gk)


def _online_softmax_step(s, vc, m, l, acc_ref):
    m_new = jnp.maximum(m, jnp.max(s, axis=0, keepdims=True))
    alpha = jnp.exp2(m - m_new)
    p = jnp.exp2(s - m_new)
    l_new = alpha * l + jnp.sum(p, axis=0, keepdims=True)
    acc_ref[...] = alpha * acc_ref[...] + _dot(vc, p.astype(BF16))
    return m_new, l_new


def _attn_a_kernel(lam_ref, qt_ref, k_ref, vt_ref, bias_ref, gsub_ref, o_ref, acc1_ref, acc2_ref):
    tq = qt_ref.shape[-1]
    nc = k_ref.shape[0]
    ratio = tq // ATTN_K_TILE
    qi = pl.program_id(2)
    qt = qt_ref[...]
    row = lax.broadcasted_iota(jnp.int32, qt.shape, 0)
    zero = jnp.zeros_like(qt)
    w1 = jnp.where(row < DIFF_QK_DIM, qt, zero)
    w2 = jnp.where(row >= DIFF_QK_DIM, qt, zero)
    acc1_ref[...] = jnp.zeros_like(acc1_ref)
    acc2_ref[...] = jnp.zeros_like(acc2_ref)

    def body(c, carry):
        m1, l1, m2, l2 = carry
        kc = k_ref[c]
        vc = vt_ref[c]
        bias = bias_ref[jnp.clip(c - qi * ratio, -2, ratio + 1) + 2]
        m1, l1 = _online_softmax_step(_dot(kc, w1) + bias, vc, m1, l1, acc1_ref)
        m2, l2 = _online_softmax_step(_dot(kc, w2) + bias, vc, m2, l2, acc2_ref)
        return m1, l1, m2, l2

    neg = jnp.full((1, tq), -jnp.inf, F32)
    zer = jnp.zeros((1, tq), F32)
    _, l1, _, l2 = lax.fori_loop(0, nc, body, (neg, zer, neg, zer))
    oa = acc1_ref[...] / l1 - lam_ref[0, 0] * (acc2_ref[...] / l2)
    ms = jnp.mean(oa * oa, axis=0, keepdims=True)
    o_ref[...] = (oa * lax.rsqrt(ms + EPS) * gsub_ref[...]).astype(BF16)


def _attn_b_kernel(qt_ref, k_ref, vt_ref, o_ref, acc_ref):
    tq = qt_ref.shape[-1]
    nc = k_ref.shape[0]
    qt = qt_ref[...]
    acc_ref[...] = jnp.zeros_like(acc_ref)

    def body(c, carry):
        m, l = carry
        return _online_softmax_step(_dot(k_ref[c], qt), vt_ref[c], m, l, acc_ref)

    neg = jnp.full((1, tq), -jnp.inf, F32)
    zer = jnp.zeros((1, tq), F32)
    _, l = lax.fori_loop(0, nc, body, (neg, zer))
    o_ref[...] = (acc_ref[...] / l).astype(BF16)


def _attn_a(lam, qt, k, vt, bias, gsub):
    b, h, _, s = qt.shape
    nc, tk = k.shape[2], k.shape[3]
    tq = ATTN_Q_TILE
    nt = bias.shape[1]
    return pl.pallas_call(
        _attn_a_kernel,
        grid=(b, h, s // tq),
        in_specs=[
            pl.BlockSpec(memory_space=pltpu.SMEM),
            pl.BlockSpec((None, None, HEAD_DIM, tq), lambda bi, hi, qi: (bi, hi, 0, qi)),
            pl.BlockSpec((None, None, nc, tk, HEAD_DIM), lambda bi, hi, qi: (bi, hi, 0, 0, 0)),
            pl.BlockSpec((None, None, nc, HEAD_DIM, tk), lambda bi, hi, qi: (bi, hi, 0, 0, 0)),
            pl.BlockSpec((None, nt, tk, tq), lambda bi, hi, qi: (hi, 0, 0, 0)),
            pl.BlockSpec((HEAD_DIM, 1), lambda bi, hi, qi: (0, 0)),
        ],
        out_specs=pl.BlockSpec((None, HEAD_DIM, tq), lambda bi, hi, qi: (bi, hi, qi)),
        out_shape=jax.ShapeDtypeStruct((b, h * HEAD_DIM, s), BF16),
        scratch_shapes=[pltpu.VMEM((HEAD_DIM, tq), F32), pltpu.VMEM((HEAD_DIM, tq), F32)],
        compiler_params=pltpu.CompilerParams(
            dimension_semantics=("arbitrary", "arbitrary", "arbitrary"),
            vmem_limit_bytes=VMEM_LIMIT_BYTES),
        name="attn_diff",
    )(lam, qt, k, vt, bias, gsub)


def _attn_b(qt, k, vt):
    b, h, _, s = qt.shape
    nc, tk = k.shape[2], k.shape[3]
    tq = ATTN_Q_TILE
    return pl.pallas_call(
        _attn_b_kernel,
        grid=(b, h, s // tq),
        in_specs=[
            pl.BlockSpec((None, None, HEAD_DIM, tq), lambda bi, hi, qi: (bi, hi, 0, qi)),
            pl.BlockSpec((None, None, nc, tk, HEAD_DIM), lambda bi, hi, qi: (bi, hi // GQA_GROUP, 0, 0, 0)),
            pl.BlockSpec((None, None, nc, HEAD_DIM, tk), lambda bi, hi, qi: (bi, hi // GQA_GROUP, 0, 0, 0)),
        ],
        out_specs=pl.BlockSpec((None, HEAD_DIM, tq), lambda bi, hi, qi: (bi, hi, qi)),
        out_shape=jax.ShapeDtypeStruct((b, h * HEAD_DIM, s), BF16),
        scratch_shapes=[pltpu.VMEM((HEAD_DIM, tq), F32)],
        compiler_params=pltpu.CompilerParams(
            dimension_semantics=("arbitrary", "arbitrary", "arbitrary"),
            vmem_limit_bytes=VMEM_LIMIT_BYTES),
        name="attn_gqa",
    )(qt, k, vt)


def _tail_kernel(x_ref, oa_ref, ob_ref, wo_ref, g2_ref, wg_ref, wu_ref, wd_ref, gf_ref, o_ref):
    tn = (((0,), (0,)), ((), ()))
    x2 = (x_ref[...]
          + lax.dot_general(oa_ref[...], wo_ref[0:D_MIX_A, :], tn, preferred_element_type=F32)
          + lax.dot_general(ob_ref[...], wo_ref[D_MIX_A:, :], tn, preferred_element_type=F32))
    h = _rms_rows(x2, g2_ref[...]).astype(BF16)
    y = x2 + 0.5 * _swiglu_acc(h, wg_ref, wu_ref, wd_ref)
    o_ref[...] = _rms_rows(y, gf_ref[...])


def _tail(x1, oat, obt, wo, g2, wg, wu, wd, gf):
    b, s, _ = x1.shape
    tm = FFN_TOKEN_TILE
    return pl.pallas_call(
        _tail_kernel,
        grid=(b, s // tm),
        in_specs=[
            pl.BlockSpec((None, tm, D_MODEL), lambda bi, si: (bi, si, 0)),
            pl.BlockSpec((None, D_MIX_A, tm), lambda bi, si: (bi, 0, si)),
            pl.BlockSpec((None, D_MIX_B, tm), lambda bi, si: (bi, 0, si)),
            _resident((D_MODEL, D_MODEL)),
            _resident((1, D_MODEL)),
            _resident((D_MODEL, D_FF)),
            _resident((D_MODEL, D_FF)),
            _resident((D_FF, D_MODEL)),
            _resident((1, D_MODEL)),
        ],
        out_specs=pl.BlockSpec((None, tm, D_MODEL), lambda bi, si: (bi, si, 0)),
        out_shape=jax.ShapeDtypeStruct((b, s, D_MODEL), F32),
        compiler_params=pltpu.CompilerParams(
            dimension_semantics=("arbitrary", "arbitrary"), vmem_limit_bytes=VMEM_LIMIT_BYTES),
        name="out_proj_ffn2",
    )(x1, oat, obt, wo, g2, wg, wu, wd, gf)


def _t5_bucket(rel):
    nb = N_BUCKETS // 2
    max_exact = nb // 2
    ret = jnp.where(rel > 0, nb, 0)
    n = jnp.abs(rel)
    nf = jnp.maximum(n, 1).astype(jnp.float32)
    large = max_exact + (jnp.log(nf / max_exact) / math.log(MAX_DISTANCE / max_exact)
                         * (nb - max_exact)).astype(jnp.int32)
    large = jnp.minimum(large, nb - 1)
    return ret + jnp.where(n < max_exact, n, large)


def _bias_tiles(rel_bias, tq, tk):
    ratio = tq // tk
    delta = jnp.arange(-2, ratio + 2, dtype=jnp.int32)[:, None, None] * tk
    j = jnp.arange(tk, dtype=jnp.int32)[None, :, None]
    i = jnp.arange(tq, dtype=jnp.int32)[None, None, :]
    bucket = _t5_bucket(delta + j - i)
    table = rel_bias.astype(F32) * LOG2E
    return jnp.moveaxis(table[bucket], -1, 0)


def _rope_tables_t(s):
    rows = s // GRID_W
    row = jnp.repeat(jnp.arange(rows, dtype=jnp.int32), GRID_W).astype(F32)
    col = jnp.tile(jnp.arange(GRID_W, dtype=jnp.int32), rows).astype(F32)
    inv = ROPE_THETA ** (-jnp.arange(0, ROPE_AXIS_DIM, 2, dtype=F32) / ROPE_AXIS_DIM)
    ang = jnp.concatenate([row[:, None] * inv[None], col[:, None] * inv[None]], axis=-1)
    return jnp.cos(ang).T, jnp.sin(ang).T


def kernel(x, ffn1_norm, ffn1_w_gate, ffn1_w_up, ffn1_w_down, mix_norm, w_in, lambda_q1, lambda_k1, lambda_q2, lambda_k2, diff_subln, q_norm, k_norm, rel_bias, w_out, ffn2_norm, ffn2_w_gate, ffn2_w_up, ffn2_w_down, final_norm):
    b, s, d = x.shape
    tq, tk = ATTN_Q_TILE, ATTN_K_TILE
    nc = s // tk
    lyr = 0
    lambda_init = 0.8 - 0.6 * math.exp(-0.3 * lyr)

    deint = np.concatenate([np.arange(0, HEAD_DIM, 2), np.arange(1, HEAD_DIM, 2)])
    col = np.arange(D_IN_PROJ)
    qb_cols = 1536 + (np.arange(N_HEADS_B)[:, None] * HEAD_DIM + deint[None, :]).reshape(-1)
    kb_cols = 2048 + (np.arange(N_KV_B)[:, None] * HEAD_DIM + deint[None, :]).reshape(-1)
    col[1536:2048] = qb_cols
    col[2048:2176] = kb_cols
    w_in_t = w_in[lyr][:, col].T.astype(BF16)
    gq = q_norm[lyr].astype(F32)[deint][:, None]
    gk = k_norm[lyr].astype(F32)[deint][:, None]
    cos_t, sin_t = _rope_tables_t(s)
    bias = _bias_tiles(rel_bias, tq, tk)
    lam = (jnp.exp(jnp.sum(lambda_q1[lyr].astype(F32) * lambda_k1[lyr].astype(F32)))
           - jnp.exp(jnp.sum(lambda_q2[lyr].astype(F32) * lambda_k2[lyr].astype(F32)))
           + lambda_init).reshape(1, 1)
    gsub = (diff_subln[lyr].astype(F32) * (1.0 - lambda_init))[:, None]

    x1 = _ffn(x.reshape(b * s, d), ffn1_norm[lyr][None, :],
              ffn1_w_gate[lyr].astype(BF16), ffn1_w_up[lyr].astype(BF16), ffn1_w_down[lyr].astype(BF16))
    x1 = x1.reshape(b, s, d)

    pt = _proj(x1, mix_norm[lyr][None, :], w_in_t, cos_t, sin_t, gq, gk)

    def heads(t, nh):
        return t.reshape(b, nh, HEAD_DIM, s)

    def key_chunks(t, nh):
        return heads(t, nh).reshape(b, nh, HEAD_DIM, nc, tk).transpose(0, 1, 3, 4, 2)

    def val_chunks(t, nh):
        return heads(t, nh).reshape(b, nh, HEAD_DIM, nc, tk).transpose(0, 1, 3, 2, 4)

    qa_t = heads(pt[:, 0:512], N_HEADS_A)
    ka = key_chunks(pt[:, 512:1024], N_HEADS_A)
    va_t = val_chunks(pt[:, 1024:1536], N_HEADS_A)
    qb_t = heads(pt[:, 1536:2048], N_HEADS_B)
    kb = key_chunks(pt[:, 2048:2176], N_KV_B)
    vb_t = val_chunks(pt[:, 2176:2304], N_KV_B)

    oa_t = _attn_a(lam, qa_t, ka, va_t, bias, gsub)
    ob_t = _attn_b(qb_t, kb, vb_t)

    return _tail(x1, oa_t, ob_t, w_out[lyr].astype(BF16), ffn2_norm[lyr][None, :],
                 ffn2_w_gate[lyr].astype(BF16), ffn2_w_up[lyr].astype(BF16), ffn2_w_down[lyr].astype(BF16),
                 final_norm[None, :])
```

```python
import functools
import math

import numpy as np
import jax
import jax.numpy as jnp
from jax import lax
from jax.experimental import pallas as pl
from jax.experimental.pallas import tpu as pltpu

D_MODEL = 1024
HEAD_DIM = 64
D_MIX_A = 512
D_MIX_B = 512
N_HEADS_A = 8
DIFF_QK_DIM = 32
N_HEADS_B = 8
GQA_GROUP = 4
N_KV_B = 2
D_FF = 2816
GRID_W = 64
ROPE_THETA = 10000.0
ROPE_AXIS_DIM = 32
N_BUCKETS = 32
MAX_DISTANCE = 128
EPS = 1e-6
D_IN_PROJ = 2304
LOG2E = math.log2(math.e)

VMEM_LIMIT_BYTES = 56 * 1024 * 1024

FFN_TOKEN_TILE = 512
FFN_CHUNKS = ((0, 1024), (1024, 2048), (2048, 2816))
PROJ_TOKEN_TILE = 512
ATTN_Q_TILE = 512
ATTN_K_TILE = 256

BF16 = jnp.bfloat16
F32 = jnp.float32


def _dot(a, b):
    return jnp.dot(a, b, preferred_element_type=F32)


def _rms_rows(x, g):
    ms = jnp.mean(x * x, axis=-1, keepdims=True)
    return x * lax.rsqrt(ms + EPS) * g


def _swiglu_acc(h, wg_ref, wu_ref, wd_ref):
    y = None
    for s, e in FFN_CHUNKS:
        g = _dot(h, wg_ref[:, s:e])
        u = _dot(h, wu_ref[:, s:e])
        a = (g / (1.0 + jnp.exp(-g)) * u).astype(BF16)
        part = _dot(a, wd_ref[s:e, :])
        y = part if y is None else y + part
    return y


def _ffn_kernel(x_ref, g_ref, wg_ref, wu_ref, wd_ref, o_ref):
    x = x_ref[...]
    h = _rms_rows(x, g_ref[...]).astype(BF16)
    o_ref[...] = x + 0.5 * _swiglu_acc(h, wg_ref, wu_ref, wd_ref)


def _resident(shape):
    return pl.BlockSpec(shape, lambda *_: (0,) * len(shape), pipeline_mode=pl.Buffered(1))


def _ffn(x2d, g, wg, wu, wd):
    n = x2d.shape[0]
    tm = FFN_TOKEN_TILE
    return pl.pallas_call(
        _ffn_kernel,
        grid=(n // tm,),
        in_specs=[
            pl.BlockSpec((tm, D_MODEL), lambda i: (i, 0)),
            _resident((1, D_MODEL)),
            _resident((D_MODEL, D_FF)),
            _resident((D_MODEL, D_FF)),
            _resident((D_FF, D_MODEL)),
        ],
        out_specs=pl.BlockSpec((tm, D_MODEL), lambda i: (i, 0)),
        out_shape=jax.ShapeDtypeStruct((n, D_MODEL), F32),
        compiler_params=pltpu.CompilerParams(
            dimension_semantics=("arbitrary",), vmem_limit_bytes=VMEM_LIMIT_BYTES),
        name="ffn1",
    )(x2d, g, wg, wu, wd)


def _proj_kernel(x_ref, g_ref, wt_ref, cos_ref, sin_ref, gq_ref, gk_ref, o_ref):
    h = _rms_rows(x_ref[...], g_ref[...]).astype(BF16)

    def section(r0, r1):
        return lax.dot_general(wt_ref[r0:r1, :], h, (((1,), (1,)), ((), ())),
                               preferred_element_type=F32)

    o_ref[0:512, :] = (section(0, 512) * (DIFF_QK_DIM ** -0.5 * LOG2E)).astype(BF16)
    o_ref[512:1536, :] = section(512, 1536).astype(BF16)

    cos = cos_ref[...]
    sin = sin_ref[...]

    def norm_rope(t, gain, scale):
        ms = jnp.mean(t * t, axis=0, keepdims=True)
        t = t * lax.rsqrt(ms + EPS) * gain
        te, to = t[:32], t[32:]
        re = te * cos - to * sin
        ro = te * sin + to * cos
        return jnp.concatenate([re, ro], axis=0) * scale

    qb = section(1536, 2048)
    for hh in range(N_HEADS_B):
        r0 = hh * HEAD_DIM
        o_ref[1536 + r0:1536 + r0 + HEAD_DIM, :] = norm_rope(
            qb[r0:r0 + HEAD_DIM], gq_ref[...], HEAD_DIM ** -0.5 * LOG2E).astype(BF16)
    kvb = section(2048, 2304)
    for hh in range(N_KV_B):
        r0 = hh * HEAD_DIM
        o_ref[2048 + r0:2048 + r0 + HEAD_DIM, :] = norm_rope(
            kvb[r0:r0 + HEAD_DIM], gk_ref[...], 1.0).astype(BF16)
    o_ref[2176:2304, :] = kvb[128:256].astype(BF16)


def _proj(x1, g, w_in_t, cos_t, sin_t, gq, gk):
    b, s, _ = x1.shape
    tm = PROJ_TOKEN_TILE
    return pl.pallas_call(
        _proj_kernel,
        grid=(b, s // tm),
        in_specs=[
            pl.BlockSpec((None, tm, D_MODEL), lambda bi, si: (bi, si, 0)),
            _resident((1, D_MODEL)),
            _resident((D_IN_PROJ, D_MODEL)),
            pl.BlockSpec((ROPE_AXIS_DIM, tm), lambda bi, si: (0, si)),
            pl.BlockSpec((ROPE_AXIS_DIM, tm), lambda bi, si: (0, si)),
            _resident((HEAD_DIM, 1)),
            _resident((HEAD_DIM, 1)),
        ],
        out_specs=pl.BlockSpec((None, D_IN_PROJ, tm), lambda bi, si: (bi, 0, si)),
        out_shape=jax.ShapeDtypeStruct((b, D_IN_PROJ, s), BF16),
        compiler_params=pltpu.CompilerParams(
            dimension_semantics=("arbitrary", "arbitrary"), vmem_limit_bytes=VMEM_LIMIT_BYTES),
        name="in_proj",
    )(x1, g, w_in_t, cos_t, sin_t, gq, gk)


def _fold8(x, op):
    rows, cols = x.shape
    return op(x.reshape(rows // 8, 8, cols), axis=0)


def _score_pass(k_ref, weights, s_refs, bias_fn):
    nc = k_ref.shape[0]
    m8 = [None] * len(weights)
    for c in range(nc):
        kc = k_ref[c]
        bias = None if bias_fn is None else bias_fn(c)
        for i, (w, s_ref) in enumerate(zip(weights, s_refs)):
            s = _dot(kc, w)
            if bias is not None:
                s = s + bias
            s_ref[c] = s
            cm = _fold8(s, jnp.max)
            m8[i] = cm if m8[i] is None else jnp.maximum(m8[i], cm)
    return [jnp.max(m, axis=0, keepdims=True) for m in m8]


def _value_pass(s_refs, maxima, vt_ref):
    nc = vt_ref.shape[0]
    l8 = [None] * len(s_refs)
    acc = [None] * len(s_refs)
    for c in range(nc):
        vc = vt_ref[c]
        for i, (s_ref, m) in enumerate(zip(s_refs, maxima)):
            p = jnp.exp2(s_ref[c] - m)
            ps = _fold8(p, jnp.sum)
            l8[i] = ps if l8[i] is None else l8[i] + ps
            pv = _dot(vc, p.astype(BF16))
            acc[i] = pv if acc[i] is None else acc[i] + pv
    return [(jnp.sum(l, axis=0, keepdims=True), a) for l, a in zip(l8, acc)]


def _attn_a_kernel(lam_ref, qt_ref, k_ref, vt_ref, bias_ref, gsub_ref, o_ref, s1_ref, s2_ref):
    tq = qt_ref.shape[-1]
    ratio = tq // ATTN_K_TILE
    qi = pl.program_id(2)
    qt = qt_ref[...]
    row = lax.broadcasted_iota(jnp.int32, qt.shape, 0)
    zero = jnp.zeros_like(qt)
    w1 = jnp.where(row < DIFF_QK_DIM, qt, zero)
    w2 = jnp.where(row >= DIFF_QK_DIM, qt, zero)

    def bias_fn(c):
        return bias_ref[jnp.clip(c - qi * ratio, -2, ratio + 1) + 2]

    maxima = _score_pass(k_ref, (w1, w2), (s1_ref, s2_ref), bias_fn)
    (l1, a1), (l2, a2) = _value_pass((s1_ref, s2_ref), maxima, vt_ref)
    oa = a1 / l1 - lam_ref[0, 0] * (a2 / l2)
    ms = jnp.mean(oa * oa, axis=0, keepdims=True)
    o_ref[...] = (oa * lax.rsqrt(ms + EPS) * gsub_ref[...]).astype(BF16)


def _attn_b_kernel(qt_ref, k_ref, vt_ref, o_ref, s_ref):
    maxima = _score_pass(k_ref, (qt_ref[...],), (s_ref,), None)
    ((l, a),) = _value_pass((s_ref,), maxima, vt_ref)
    o_ref[...] = (a / l).astype(BF16)


def _attn_a(lam, qt, k, vt, bias, gsub):
    b, h, _, s = qt.shape
    nc, tk = k.shape[2], k.shape[3]
    tq = ATTN_Q_TILE
    nt = bias.shape[1]
    return pl.pallas_call(
        _attn_a_kernel,
        grid=(b, h, s // tq),
        in_specs=[
            pl.BlockSpec(memory_space=pltpu.SMEM),
            pl.BlockSpec((None, None, HEAD_DIM, tq), lambda bi, hi, qi: (bi, hi, 0, qi)),
            pl.BlockSpec((None, None, nc, tk, HEAD_DIM), lambda bi, hi, qi: (bi, hi, 0, 0, 0)),
            pl.BlockSpec((None, None, nc, HEAD_DIM, tk), lambda bi, hi, qi: (bi, hi, 0, 0, 0)),
            pl.BlockSpec((None, nt, tk, tq), lambda bi, hi, qi: (hi, 0, 0, 0)),
            pl.BlockSpec((HEAD_DIM, 1), lambda bi, hi, qi: (0, 0)),
        ],
        out_specs=pl.BlockSpec((None, HEAD_DIM, tq), lambda bi, hi, qi: (bi, hi, qi)),
        out_shape=jax.ShapeDtypeStruct((b, h * HEAD_DIM, s), BF16),
        scratch_shapes=[pltpu.VMEM((nc, tk, tq), F32), pltpu.VMEM((nc, tk, tq), F32)],
        compiler_params=pltpu.CompilerParams(
            dimension_semantics=("arbitrary", "arbitrary", "arbitrary"),
            vmem_limit_bytes=VMEM_LIMIT_BYTES),
        name="attn_diff",
    )(lam, qt, k, vt, bias, gsub)


def _attn_b(qt, k, vt):
    b, h, _, s = qt.shape
    nc, tk = k.shape[2], k.shape[3]
    tq = ATTN_Q_TILE
    return pl.pallas_call(
        _attn_b_kernel,
        grid=(b, h, s // tq),
        in_specs=[
            pl.BlockSpec((None, None, HEAD_DIM, tq), lambda bi, hi, qi: (bi, hi, 0, qi)),
            pl.BlockSpec((None, None, nc, tk, HEAD_DIM), lambda bi, hi, qi: (bi, hi // GQA_GROUP, 0, 0, 0)),
            pl.BlockSpec((None, None, nc, HEAD_DIM, tk), lambda bi, hi, qi: (bi, hi // GQA_GROUP, 0, 0, 0)),
        ],
        out_specs=pl.BlockSpec((None, HEAD_DIM, tq), lambda bi, hi, qi: (bi, hi, qi)),
        out_shape=jax.ShapeDtypeStruct((b, h * HEAD_DIM, s), BF16),
        scratch_shapes=[pltpu.VMEM((nc, tk, tq), F32)],
        compiler_params=pltpu.CompilerParams(
            dimension_semantics=("arbitrary", "arbitrary", "arbitrary"),
            vmem_limit_bytes=VMEM_LIMIT_BYTES),
        name="attn_gqa",
    )(qt, k, vt)


def _tail_kernel(x_ref, oa_ref, ob_ref, wo_ref, g2_ref, wg_ref, wu_ref, wd_ref, gf_ref, o_ref):
    tn = (((0,), (0,)), ((), ()))
    x2 = (x_ref[...]
          + lax.dot_general(oa_ref[...], wo_ref[0:D_MIX_A, :], tn, preferred_element_type=F32)
          + lax.dot_general(ob_ref[...], wo_ref[D_MIX_A:, :], tn, preferred_element_type=F32))
    h = _rms_rows(x2, g2_ref[...]).astype(BF16)
    y = x2 + 0.5 * _swiglu_acc(h, wg_ref, wu_ref, wd_ref)
    o_ref[...] = _rms_rows(y, gf_ref[...])


def _tail(x1, oat, obt, wo, g2, wg, wu, wd, gf):
    b, s, _ = x1.shape
    tm = FFN_TOKEN_TILE
    return pl.pallas_call(
        _tail_kernel,
        grid=(b, s // tm),
        in_specs=[
            pl.BlockSpec((None, tm, D_MODEL), lambda bi, si: (bi, si, 0)),
            pl.BlockSpec((None, D_MIX_A, tm), lambda bi, si: (bi, 0, si)),
            pl.BlockSpec((None, D_MIX_B, tm), lambda bi, si: (bi, 0, si)),
            _resident((D_MODEL, D_MODEL)),
            _resident((1, D_MODEL)),
            _resident((D_MODEL, D_FF)),
            _resident((D_MODEL, D_FF)),
            _resident((D_FF, D_MODEL)),
            _resident((1, D_MODEL)),
        ],
        out_specs=pl.BlockSpec((None, tm, D_MODEL), lambda bi, si: (bi, si, 0)),
        out_shape=jax.ShapeDtypeStruct((b, s, D_MODEL), F32),
        compiler_params=pltpu.CompilerParams(
            dimension_semantics=("arbitrary", "arbitrary"), vmem_limit_bytes=VMEM_LIMIT_BYTES),
        name="out_proj_ffn2",
    )(x1, oat, obt, wo, g2, wg, wu, wd, gf)


def _t5_bucket(rel):
    nb = N_BUCKETS // 2
    max_exact = nb // 2
    ret = jnp.where(rel > 0, nb, 0)
    n = jnp.abs(rel)
    nf = jnp.maximum(n, 1).astype(jnp.float32)
    large = max_exact + (jnp.log(nf / max_exact) / math.log(MAX_DISTANCE / max_exact)
                         * (nb - max_exact)).astype(jnp.int32)
    large = jnp.minimum(large, nb - 1)
    return ret + jnp.where(n < max_exact, n, large)


def _bias_tiles(rel_bias, tq, tk):
    nt = tq // tk + 4
    span = nt * tk
    period = span + tq
    rel = jnp.arange(period, dtype=jnp.int32) - (2 * tk + tq - 1)
    v = (rel_bias.astype(F32) * LOG2E)[_t5_bucket(rel)].T
    h = v.shape[0]
    skew = jnp.tile(v, (1, tq))[:, :tq * (period - 1)].reshape(h, tq, period - 1)
    rows = skew[:, :, tq - 1:tq - 1 + span]
    return rows.reshape(h, tq, nt, tk).transpose(0, 2, 3, 1)


def _rope_tables_t(s):
    rows = s // GRID_W
    row = jnp.repeat(jnp.arange(rows, dtype=jnp.int32), GRID_W).astype(F32)
    col = jnp.tile(jnp.arange(GRID_W, dtype=jnp.int32), rows).astype(F32)
    inv = ROPE_THETA ** (-jnp.arange(0, ROPE_AXIS_DIM, 2, dtype=F32) / ROPE_AXIS_DIM)
    ang = jnp.concatenate([row[:, None] * inv[None], col[:, None] * inv[None]], axis=-1)
    return jnp.cos(ang).T, jnp.sin(ang).T


def kernel(x, ffn1_norm, ffn1_w_gate, ffn1_w_up, ffn1_w_down, mix_norm, w_in, lambda_q1, lambda_k1, lambda_q2, lambda_k2, diff_subln, q_norm, k_norm, rel_bias, w_out, ffn2_norm, ffn2_w_gate, ffn2_w_up, ffn2_w_down, final_norm):
    b, s, d = x.shape
    tq, tk = ATTN_Q_TILE, ATTN_K_TILE
    nc = s // tk
    lyr = 0
    lambda_init = 0.8 - 0.6 * math.exp(-0.3 * lyr)

    deint = np.concatenate([np.arange(0, HEAD_DIM, 2), np.arange(1, HEAD_DIM, 2)])
    col = np.arange(D_IN_PROJ)
    qb_cols = 1536 + (np.arange(N_HEADS_B)[:, None] * HEAD_DIM + deint[None, :]).reshape(-1)
    kb_cols = 2048 + (np.arange(N_KV_B)[:, None] * HEAD_DIM + deint[None, :]).reshape(-1)
    col[1536:2048] = qb_cols
    col[2048:2176] = kb_cols
    w_in_t = w_in[lyr][:, col].T.astype(BF16)
    gq = q_norm[lyr].astype(F32)[deint][:, None]
    gk = k_norm[lyr].astype(F32)[deint][:, None]
    cos_t, sin_t = _rope_tables_t(s)
    bias = _bias_tiles(rel_bias, tq, tk)
    lam = (jnp.exp(jnp.sum(lambda_q1[lyr].astype(F32) * lambda_k1[lyr].astype(F32)))
           - jnp.exp(jnp.sum(lambda_q2[lyr].astype(F32) * lambda_k2[lyr].astype(F32)))
           + lambda_init).reshape(1, 1)
    gsub = (diff_subln[lyr].astype(F32) * (1.0 - lambda_init))[:, None]

    x1 = _ffn(x.reshape(b * s, d), ffn1_norm[lyr][None, :],
              ffn1_w_gate[lyr].astype(BF16), ffn1_w_up[lyr].astype(BF16), ffn1_w_down[lyr].astype(BF16))
    x1 = x1.reshape(b, s, d)

    pt = _proj(x1, mix_norm[lyr][None, :], w_in_t, cos_t, sin_t, gq, gk)

    def heads(t, nh):
        return t.reshape(b, nh, HEAD_DIM, s)

    def key_chunks(t, nh):
        return heads(t, nh).reshape(b, nh, HEAD_DIM, nc, tk).transpose(0, 1, 3, 4, 2)

    def val_chunks(t, nh):
        return heads(t, nh).reshape(b, nh, HEAD_DIM, nc, tk).transpose(0, 1, 3, 2, 4)

    qa_t = heads(pt[:, 0:512], N_HEADS_A)
    ka = key_chunks(pt[:, 512:1024], N_HEADS_A)
    va_t = val_chunks(pt[:, 1024:1536], N_HEADS_A)
    qb_t = heads(pt[:, 1536:2048], N_HEADS_B)
    kb = key_chunks(pt[:, 2048:2176], N_KV_B)
    vb_t = val_chunks(pt[:, 2176:2304], N_KV_B)

    oa_t = _attn_a(lam, qa_t, ka, va_t, bias, gsub)
    ob_t = _attn_b(qb_t, kb, vb_t)

    return _tail(x1, oa_t, ob_t, w_out[lyr].astype(BF16), ffn2_norm[lyr][None, :],
                 ffn2_w_gate[lyr].astype(BF16), ffn2_w_up[lyr].astype(BF16), ffn2_w_down[lyr].astype(BF16),
                 final_norm[None, :])
```

```python
import math

import numpy as np
import jax
import jax.numpy as jnp
from jax import lax
from jax.experimental import pallas as pl
from jax.experimental.pallas import tpu as pltpu

D_MODEL = 1024
HEAD_DIM = 64
D_MIX_A = 512
D_MIX_B = 512
N_HEADS_A = 8
DIFF_QK_DIM = 32
N_HEADS_B = 8
GQA_GROUP = 4
N_KV_B = 2
D_FF = 2816
GRID_W = 64
ROPE_THETA = 10000.0
ROPE_AXIS_DIM = 32
N_BUCKETS = 32
MAX_DISTANCE = 128
EPS = 1e-6
D_IN_PROJ = 2304
LOG2E = math.log2(math.e)

VMEM_LIMIT_BYTES = 56 * 1024 * 1024
BF16_SUBLANE_TILE = 16

FFN_TOKEN_TILE = 512
FFN_CHUNKS = ((0, 1024), (1024, 2048), (2048, 2816))
PROJ_TOKEN_TILE = 512
ATTN_Q_TILE = 512
ATTN_K_TILE = 256
ATTN_A_SUBTILES = 2
V_ROWS = HEAD_DIM + BF16_SUBLANE_TILE

BF16 = jnp.bfloat16
F32 = jnp.float32


def _dot(a, b):
    return jnp.dot(a, b, preferred_element_type=F32)


def _rms_rows(x, g):
    ms = jnp.mean(x * x, axis=-1, keepdims=True)
    return x * lax.rsqrt(ms + EPS) * g


def _swiglu_acc(h, wg_ref, wu_ref, wd_ref):
    y = None
    for s, e in FFN_CHUNKS:
        g = _dot(h, wg_ref[:, s:e])
        u = _dot(h, wu_ref[:, s:e])
        a = (g / (1.0 + jnp.exp(-g)) * u).astype(BF16)
        part = _dot(a, wd_ref[s:e, :])
        y = part if y is None else y + part
    return y


def _resident(shape):
    return pl.BlockSpec(shape, lambda *_: (0,) * len(shape), pipeline_mode=pl.Buffered(1))


def _params(n_grid_axes):
    return pltpu.CompilerParams(dimension_semantics=("arbitrary",) * n_grid_axes,
                                vmem_limit_bytes=VMEM_LIMIT_BYTES)


def _ffn_kernel(x_ref, g_ref, wg_ref, wu_ref, wd_ref, o_ref):
    x = x_ref[...]
    h = _rms_rows(x, g_ref[...]).astype(BF16)
    o_ref[...] = x + 0.5 * _swiglu_acc(h, wg_ref, wu_ref, wd_ref)


def _ffn(x2d, g, wg, wu, wd):
    n = x2d.shape[0]
    tm = FFN_TOKEN_TILE
    return pl.pallas_call(
        _ffn_kernel,
        grid=(n // tm,),
        in_specs=[
            pl.BlockSpec((tm, D_MODEL), lambda i: (i, 0)),
            _resident((1, D_MODEL)),
            _resident((D_MODEL, D_FF)),
            _resident((D_MODEL, D_FF)),
            _resident((D_FF, D_MODEL)),
        ],
        out_specs=pl.BlockSpec((tm, D_MODEL), lambda i: (i, 0)),
        out_shape=jax.ShapeDtypeStruct((n, D_MODEL), F32),
        compiler_params=_params(1),
        name="ffn1",
    )(x2d, g, wg, wu, wd)


def _proj_kernel(x_ref, g_ref, wt_ref, cos_ref, sin_ref, gq_ref, gk_ref, o_ref):
    h = _rms_rows(x_ref[...], g_ref[...]).astype(BF16)

    def section(r0, r1):
        return lax.dot_general(wt_ref[r0:r1, :], h, (((1,), (1,)), ((), ())),
                               preferred_element_type=F32)

    o_ref[0:512, :] = (section(0, 512) * (DIFF_QK_DIM ** -0.5 * LOG2E)).astype(BF16)
    o_ref[512:1536, :] = section(512, 1536).astype(BF16)

    cos = cos_ref[...]
    sin = sin_ref[...]

    def norm_rope(t, gain, scale):
        ms = jnp.mean(t * t, axis=0, keepdims=True)
        t = t * lax.rsqrt(ms + EPS) * gain
        te, to = t[:32], t[32:]
        re = te * cos - to * sin
        ro = te * sin + to * cos
        return jnp.concatenate([re, ro], axis=0) * scale

    qb = section(1536, 2048)
    for hh in range(N_HEADS_B):
        r0 = hh * HEAD_DIM
        o_ref[1536 + r0:1536 + r0 + HEAD_DIM, :] = norm_rope(
            qb[r0:r0 + HEAD_DIM], gq_ref[...], HEAD_DIM ** -0.5 * LOG2E).astype(BF16)
    kvb = section(2048, 2304)
    for hh in range(N_KV_B):
        r0 = hh * HEAD_DIM
        o_ref[2048 + r0:2048 + r0 + HEAD_DIM, :] = norm_rope(
            kvb[r0:r0 + HEAD_DIM], gk_ref[...], 1.0).astype(BF16)
    o_ref[2176:2304, :] = kvb[128:256].astype(BF16)


def _proj(x1, g, w_in_t, cos_t, sin_t, gq, gk):
    b, s, _ = x1.shape
    tm = PROJ_TOKEN_TILE
    return pl.pallas_call(
        _proj_kernel,
        grid=(b, s // tm),
        in_specs=[
            pl.BlockSpec((None, tm, D_MODEL), lambda bi, si: (bi, si, 0)),
            _resident((1, D_MODEL)),
            _resident((D_IN_PROJ, D_MODEL)),
            pl.BlockSpec((ROPE_AXIS_DIM, tm), lambda bi, si: (0, si)),
            pl.BlockSpec((ROPE_AXIS_DIM, tm), lambda bi, si: (0, si)),
            _resident((HEAD_DIM, 1)),
            _resident((HEAD_DIM, 1)),
        ],
        out_specs=pl.BlockSpec((None, D_IN_PROJ, tm), lambda bi, si: (bi, 0, si)),
        out_shape=jax.ShapeDtypeStruct((b, D_IN_PROJ, s), BF16),
        compiler_params=_params(2),
        name="in_proj",
    )(x1, g, w_in_t, cos_t, sin_t, gq, gk)


def _toeplitz_kernel(base_ref, o_ref):
    tk, tq = o_ref.shape
    rows = jnp.broadcast_to(base_ref[...], (tk, tq + tk))
    skew = pltpu.roll(rows, 0, 1, stride=1, stride_axis=0)
    o_ref[...] = skew[:, tk:tk + tq]


def _bias_tiles(base, tq, tk):
    h, nt = base.shape[:2]
    return pl.pallas_call(
        _toeplitz_kernel,
        grid=(h, nt),
        in_specs=[pl.BlockSpec((None, None, 1, tq + tk), lambda hi, di: (hi, di, 0, 0))],
        out_specs=pl.BlockSpec((None, None, tk, tq), lambda hi, di: (hi, di, 0, 0)),
        out_shape=jax.ShapeDtypeStruct((h, nt, tk, tq), F32),
        compiler_params=_params(2),
        name="bias_tiles",
    )(base)


def _fold8(x, op):
    rows, cols = x.shape
    return op(x.reshape(rows // 8, 8, cols), axis=0)


def _attention_streams(k_ref, vt_ref, weights, bias_fns, s_ref, chunk0):
    nc = k_ref.shape[0]
    n = len(weights)
    maxima = [None] * n
    acc = [None] * n
    for phase in range(n + 1):
        m8 = None
        for c in range(nc):
            if phase >= 1:
                s_prev = s_ref[chunk0 + c]
            if phase < n:
                s = _dot(k_ref[c], weights[phase])
                if bias_fns[phase] is not None:
                    s = s + bias_fns[phase](c)
                s_ref[chunk0 + c] = s
                cm = _fold8(s, jnp.max)
                m8 = cm if m8 is None else jnp.maximum(m8, cm)
            if phase >= 1:
                i = phase - 1
                p = jnp.exp2(s_prev - maxima[i])
                pv = _dot(vt_ref[c], p.astype(BF16))
                acc[i] = pv if acc[i] is None else acc[i] + pv
        if phase < n:
            maxima[phase] = jnp.max(m8, axis=0, keepdims=True)
    return acc


def _normalized(acc):
    return acc[:HEAD_DIM] / acc[HEAD_DIM:HEAD_DIM + 1]


def _attn_a_kernel(lam_ref, zero_ref, qt_ref, k_ref, vt_ref, bias_ref, gsub_ref, o_ref, s_ref):
    tq = ATTN_Q_TILE
    ratio = tq // ATTN_K_TILE
    qi = pl.program_id(2)
    weights, bias_fns = [], []
    for t in range(ATTN_A_SUBTILES):
        qt = qt_ref[:, t * tq:(t + 1) * tq]
        row = lax.broadcasted_iota(jnp.int32, qt.shape, 0)
        zero = jnp.zeros_like(qt)

        def bias_fn(c, t=t):
            first = (qi * ATTN_A_SUBTILES + t) * ratio
            return bias_ref[jnp.clip(c - first, -2, ratio + 1) + 2]

        weights += [jnp.where(row < DIFF_QK_DIM, qt, zero), jnp.where(row >= DIFF_QK_DIM, qt, zero)]
        bias_fns += [bias_fn, bias_fn]
    acc = _attention_streams(k_ref, vt_ref, weights, bias_fns, s_ref, zero_ref[0])
    for t in range(ATTN_A_SUBTILES):
        oa = _normalized(acc[2 * t]) - lam_ref[0, 0] * _normalized(acc[2 * t + 1])
        ms = jnp.mean(oa * oa, axis=0, keepdims=True)
        o_ref[:, t * tq:(t + 1) * tq] = (oa * lax.rsqrt(ms + EPS) * gsub_ref[...]).astype(BF16)


def _attn_b_kernel(zero_ref, qt_ref, k_ref, vt_ref, o_ref, s_ref):
    weights = [qt_ref[g] for g in range(GQA_GROUP)]
    acc = _attention_streams(k_ref, vt_ref, weights, [None] * GQA_GROUP, s_ref, zero_ref[0])
    for g in range(GQA_GROUP):
        o_ref[g * HEAD_DIM:(g + 1) * HEAD_DIM, :] = _normalized(acc[g]).astype(BF16)


def _attn_a(lam, qt, k, vt, bias, gsub):
    b, h, _, s = qt.shape
    nc, tk = k.shape[2], k.shape[3]
    tq = ATTN_Q_TILE
    tqs = tq * ATTN_A_SUBTILES
    nt = bias.shape[1]
    return pl.pallas_call(
        _attn_a_kernel,
        grid=(b, h, s // tqs),
        in_specs=[
            pl.BlockSpec(memory_space=pltpu.SMEM),
            pl.BlockSpec(memory_space=pltpu.SMEM),
            pl.BlockSpec((None, None, HEAD_DIM, tqs), lambda bi, hi, qi: (bi, hi, 0, qi)),
            pl.BlockSpec((None, None, nc, tk, HEAD_DIM), lambda bi, hi, qi: (bi, hi, 0, 0, 0)),
            pl.BlockSpec((None, None, nc, V_ROWS, tk), lambda bi, hi, qi: (bi, hi, 0, 0, 0)),
            pl.BlockSpec((None, nt, tk, tq), lambda bi, hi, qi: (hi, 0, 0, 0)),
            pl.BlockSpec((HEAD_DIM, 1), lambda bi, hi, qi: (0, 0)),
        ],
        out_specs=pl.BlockSpec((None, HEAD_DIM, tqs), lambda bi, hi, qi: (bi, hi, qi)),
        out_shape=jax.ShapeDtypeStruct((b, h * HEAD_DIM, s), BF16),
        scratch_shapes=[pltpu.VMEM((nc, tk, tq), F32)],
        compiler_params=_params(3),
        name="attn_diff",
    )(lam, jnp.zeros((1,), jnp.int32), qt, k, vt, bias, gsub)


def _attn_b(qt, k, vt):
    b, h, _, s = qt.shape
    nkv = k.shape[1]
    nc, tk = k.shape[2], k.shape[3]
    tq = ATTN_Q_TILE
    qt = qt.reshape(b, nkv, GQA_GROUP, HEAD_DIM, s)
    return pl.pallas_call(
        _attn_b_kernel,
        grid=(b, nkv, s // tq),
        in_specs=[
            pl.BlockSpec(memory_space=pltpu.SMEM),
            pl.BlockSpec((None, None, GQA_GROUP, HEAD_DIM, tq), lambda bi, gi, qi: (bi, gi, 0, 0, qi)),
            pl.BlockSpec((None, None, nc, tk, HEAD_DIM), lambda bi, gi, qi: (bi, gi, 0, 0, 0)),
            pl.BlockSpec((None, None, nc, V_ROWS, tk), lambda bi, gi, qi: (bi, gi, 0, 0, 0)),
        ],
        out_specs=pl.BlockSpec((None, GQA_GROUP * HEAD_DIM, tq), lambda bi, gi, qi: (bi, gi, qi)),
        out_shape=jax.ShapeDtypeStruct((b, h * HEAD_DIM, s), BF16),
        scratch_shapes=[pltpu.VMEM((nc, tk, tq), F32)],
        compiler_params=_params(3),
        name="attn_gqa",
    )(jnp.zeros((1,), jnp.int32), qt, k, vt)


def _tail_kernel(x_ref, oa_ref, ob_ref, wo_ref, g2_ref, wg_ref, wu_ref, wd_ref, gf_ref, o_ref):
    tn = (((0,), (0,)), ((), ()))
    x2 = (x_ref[...]
          + lax.dot_general(oa_ref[...], wo_ref[0:D_MIX_A, :], tn, preferred_element_type=F32)
          + lax.dot_general(ob_ref[...], wo_ref[D_MIX_A:, :], tn, preferred_element_type=F32))
    h = _rms_rows(x2, g2_ref[...]).astype(BF16)
    y = x2 + 0.5 * _swiglu_acc(h, wg_ref, wu_ref, wd_ref)
    o_ref[...] = _rms_rows(y, gf_ref[...])


def _tail(x1, oat, obt, wo, g2, wg, wu, wd, gf):
    b, s, _ = x1.shape
    tm = FFN_TOKEN_TILE
    return pl.pallas_call(
        _tail_kernel,
        grid=(b, s // tm),
        in_specs=[
            pl.BlockSpec((None, tm, D_MODEL), lambda bi, si: (bi, si, 0)),
            pl.BlockSpec((None, D_MIX_A, tm), lambda bi, si: (bi, 0, si)),
            pl.BlockSpec((None, D_MIX_B, tm), lambda bi, si: (bi, 0, si)),
            _resident((D_MODEL, D_MODEL)),
            _resident((1, D_MODEL)),
            _resident((D_MODEL, D_FF)),
            _resident((D_MODEL, D_FF)),
            _resident((D_FF, D_MODEL)),
            _resident((1, D_MODEL)),
        ],
        out_specs=pl.BlockSpec((None, tm, D_MODEL), lambda bi, si: (bi, si, 0)),
        out_shape=jax.ShapeDtypeStruct((b, s, D_MODEL), F32),
        compiler_params=_params(2),
        name="out_proj_ffn2",
    )(x1, oat, obt, wo, g2, wg, wu, wd, gf)


def _t5_bucket(rel):
    nb = N_BUCKETS // 2
    max_exact = nb // 2
    ret = jnp.where(rel > 0, nb, 0)
    n = jnp.abs(rel)
    nf = jnp.maximum(n, 1).astype(jnp.float32)
    large = max_exact + (jnp.log(nf / max_exact) / math.log(MAX_DISTANCE / max_exact)
                         * (nb - max_exact)).astype(jnp.int32)
    large = jnp.minimum(large, nb - 1)
    return ret + jnp.where(n < max_exact, n, large)


def _bias_base_rows(rel_bias, tq, tk):
    nt = tq // tk + 4
    delta = jnp.arange(-2, nt - 2, dtype=jnp.int32)[:, None] * tk
    y = jnp.arange(tq + tk, dtype=jnp.int32)[None, :]
    table = rel_bias.astype(F32) * LOG2E
    base = table[_t5_bucket(delta + tk - y)]
    return jnp.moveaxis(base, -1, 0)[:, :, None, :]


def _rope_tables_t(s):
    rows = s // GRID_W
    row = jnp.repeat(jnp.arange(rows, dtype=jnp.int32), GRID_W).astype(F32)
    col = jnp.tile(jnp.arange(GRID_W, dtype=jnp.int32), rows).astype(F32)
    inv = ROPE_THETA ** (-jnp.arange(0, ROPE_AXIS_DIM, 2, dtype=F32) / ROPE_AXIS_DIM)
    ang = jnp.concatenate([row[:, None] * inv[None], col[:, None] * inv[None]], axis=-1)
    return jnp.cos(ang).T, jnp.sin(ang).T


def kernel(x, ffn1_norm, ffn1_w_gate, ffn1_w_up, ffn1_w_down, mix_norm, w_in, lambda_q1, lambda_k1, lambda_q2, lambda_k2, diff_subln, q_norm, k_norm, rel_bias, w_out, ffn2_norm, ffn2_w_gate, ffn2_w_up, ffn2_w_down, final_norm):
    b, s, d = x.shape
    tq, tk = ATTN_Q_TILE, ATTN_K_TILE
    nc = s // tk
    lyr = 0
    lambda_init = 0.8 - 0.6 * math.exp(-0.3 * lyr)

    deint = np.concatenate([np.arange(0, HEAD_DIM, 2), np.arange(1, HEAD_DIM, 2)])
    col = np.arange(D_IN_PROJ)
    qb_cols = 1536 + (np.arange(N_HEADS_B)[:, None] * HEAD_DIM + deint[None, :]).reshape(-1)
    kb_cols = 2048 + (np.arange(N_KV_B)[:, None] * HEAD_DIM + deint[None, :]).reshape(-1)
    col[1536:2048] = qb_cols
    col[2048:2176] = kb_cols
    w_in_t = w_in[lyr][:, col].T.astype(BF16)
    gq = q_norm[lyr].astype(F32)[deint][:, None]
    gk = k_norm[lyr].astype(F32)[deint][:, None]
    cos_t, sin_t = _rope_tables_t(s)
    bias = _bias_tiles(_bias_base_rows(rel_bias, tq, tk), tq, tk)
    lam = (jnp.exp(jnp.sum(lambda_q1[lyr].astype(F32) * lambda_k1[lyr].astype(F32)))
           - jnp.exp(jnp.sum(lambda_q2[lyr].astype(F32) * lambda_k2[lyr].astype(F32)))
           + lambda_init).reshape(1, 1)
    gsub = (diff_subln[lyr].astype(F32) * (1.0 - lambda_init))[:, None]

    x1 = _ffn(x.reshape(b * s, d), ffn1_norm[lyr][None, :],
              ffn1_w_gate[lyr].astype(BF16), ffn1_w_up[lyr].astype(BF16), ffn1_w_down[lyr].astype(BF16))
    x1 = x1.reshape(b, s, d)

    pt = _proj(x1, mix_norm[lyr][None, :], w_in_t, cos_t, sin_t, gq, gk)

    def heads(t, nh):
        return t.reshape(b, nh, HEAD_DIM, s)

    def key_chunks(t, nh):
        return heads(t, nh).reshape(b, nh, HEAD_DIM, nc, tk).transpose(0, 1, 3, 4, 2)

    def val_chunks(t, nh):
        v = heads(t, nh).reshape(b, nh, HEAD_DIM, nc, tk).transpose(0, 1, 3, 2, 4)
        ones = jnp.ones((b, nh, nc, V_ROWS - HEAD_DIM, tk), BF16)
        return jnp.concatenate([v, ones], axis=3)

    qa_t = heads(pt[:, 0:512], N_HEADS_A)
    ka = key_chunks(pt[:, 512:1024], N_HEADS_A)
    va_t = val_chunks(pt[:, 1024:1536], N_HEADS_A)
    qb_t = heads(pt[:, 1536:2048], N_HEADS_B)
    kb = key_chunks(pt[:, 2048:2176], N_KV_B)
    vb_t = val_chunks(pt[:, 2176:2304], N_KV_B)

    oa_t = _attn_a(lam, qa_t, ka, va_t, bias, gsub)
    ob_t = _attn_b(qb_t, kb, vb_t)

    return _tail(x1, oa_t, ob_t, w_out[lyr].astype(BF16), ffn2_norm[lyr][None, :],
                 ffn2_w_gate[lyr].astype(BF16), ffn2_w_up[lyr].astype(BF16), ffn2_w_down[lyr].astype(BF16),
                 final_norm[None, :])
```

```python
import math

import numpy as np
import jax
import jax.numpy as jnp
from jax import lax
from jax.experimental import pallas as pl
from jax.experimental.pallas import tpu as pltpu

D_MODEL = 1024
HEAD_DIM = 64
D_MIX_A = 512
D_MIX_B = 512
N_HEADS_A = 8
DIFF_QK_DIM = 32
N_HEADS_B = 8
GQA_GROUP = 4
N_KV_B = 2
D_FF = 2816
GRID_W = 64
ROPE_THETA = 10000.0
ROPE_AXIS_DIM = 32
N_BUCKETS = 32
MAX_DISTANCE = 128
EPS = 1e-6
D_IN_PROJ = 2304
LOG2E = math.log2(math.e)

VMEM_LIMIT_BYTES = 56 * 1024 * 1024
BF16_SUBLANE_TILE = 16

FFN_TOKEN_TILE = 512
FFN_CHUNKS = ((0, 1024), (1024, 2048), (2048, 2816))
ATTN_Q_TILE = 512
ATTN_K_TILE = 256
V_ROWS = HEAD_DIM + BF16_SUBLANE_TILE

BF16 = jnp.bfloat16
F32 = jnp.float32


def _dot(a, b):
    return jnp.dot(a, b, preferred_element_type=F32)


def _rms_rows(x, g):
    ms = jnp.mean(x * x, axis=-1, keepdims=True)
    return x * lax.rsqrt(ms + EPS) * g


def _swiglu_acc(h, wg_ref, wu_ref, wd_ref):
    y = None
    for s, e in FFN_CHUNKS:
        g = _dot(h, wg_ref[:, s:e])
        u = _dot(h, wu_ref[:, s:e])
        a = (g / (1.0 + jnp.exp(-g)) * u).astype(BF16)
        part = _dot(a, wd_ref[s:e, :])
        y = part if y is None else y + part
    return y


def _resident(shape):
    return pl.BlockSpec(shape, lambda *_: (0,) * len(shape), pipeline_mode=pl.Buffered(1))


def _params(n_grid_axes):
    return pltpu.CompilerParams(dimension_semantics=("arbitrary",) * n_grid_axes,
                                vmem_limit_bytes=VMEM_LIMIT_BYTES)


def _toeplitz_kernel(base_ref, o_ref):
    tk, tq = o_ref.shape
    rows = jnp.broadcast_to(base_ref[...], (tk, tq + tk))
    skew = pltpu.roll(rows, 0, 1, stride=1, stride_axis=0)
    o_ref[...] = skew[:, tk:tk + tq]


def _bias_tiles(base, tq, tk):
    h, nt = base.shape[:2]
    return pl.pallas_call(
        _toeplitz_kernel,
        grid=(h, nt),
        in_specs=[pl.BlockSpec((None, None, 1, tq + tk), lambda hi, di: (hi, di, 0, 0))],
        out_specs=pl.BlockSpec((None, None, tk, tq), lambda hi, di: (hi, di, 0, 0)),
        out_shape=jax.ShapeDtypeStruct((h, nt, tk, tq), F32),
        compiler_params=_params(2),
        name="bias_tiles",
    )(base)


def _ffn_kernel(x_ref, g_ref, wg_ref, wu_ref, wd_ref, o_ref):
    x = x_ref[...]
    h = _rms_rows(x, g_ref[...]).astype(BF16)
    o_ref[...] = x + 0.5 * _swiglu_acc(h, wg_ref, wu_ref, wd_ref)


def _ffn(x2d, g, wg, wu, wd):
    n = x2d.shape[0]
    tm = FFN_TOKEN_TILE
    return pl.pallas_call(
        _ffn_kernel,
        grid=(n // tm,),
        in_specs=[
            pl.BlockSpec((tm, D_MODEL), lambda i: (i, 0)),
            _resident((1, D_MODEL)),
            _resident((D_MODEL, D_FF)),
            _resident((D_MODEL, D_FF)),
            _resident((D_FF, D_MODEL)),
        ],
        out_specs=pl.BlockSpec((tm, D_MODEL), lambda i: (i, 0)),
        out_shape=jax.ShapeDtypeStruct((n, D_MODEL), F32),
        compiler_params=_params(1),
        name="ffn1",
    )(x2d, g, wg, wu, wd)


def _proj_kernel(x_ref, g_ref, wt_ref, cos_ref, sin_ref, gq_ref, gk_ref,
                 qa_ref, ka_ref, va_ref, qb_ref, kb_ref, vb_ref):
    tk = ATTN_K_TILE
    n_chunks = x_ref.shape[0] // tk
    h = _rms_rows(x_ref[...], g_ref[...]).astype(BF16)
    cos = cos_ref[...]
    sin = sin_ref[...]
    ones = jnp.ones((V_ROWS - HEAD_DIM, tk), BF16)

    def section(r0, r1):
        return lax.dot_general(wt_ref[r0:r1, :], h, (((1,), (1,)), ((), ())),
                               preferred_element_type=F32)

    def norm_rope(t, gain, scale):
        ms = jnp.mean(t * t, axis=0, keepdims=True)
        t = t * lax.rsqrt(ms + EPS) * gain
        te, to = t[:32], t[32:]
        re = te * cos - to * sin
        ro = te * sin + to * cos
        return jnp.concatenate([re, ro], axis=0) * scale

    def put_keys(dst_ref, head, kt):
        for j in range(n_chunks):
            dst_ref[head, j] = kt[:, j * tk:(j + 1) * tk].T.astype(BF16)

    def put_values(dst_ref, head, vt):
        for j in range(n_chunks):
            dst_ref[head, j, 0:HEAD_DIM, :] = vt[:, j * tk:(j + 1) * tk].astype(BF16)
            dst_ref[head, j, HEAD_DIM:V_ROWS, :] = ones

    def head_rows(t, head):
        return t[head * HEAD_DIM:(head + 1) * HEAD_DIM]

    qa = section(0, 512) * (DIFF_QK_DIM ** -0.5 * LOG2E)
    for hh in range(N_HEADS_A):
        qa_ref[hh] = head_rows(qa, hh).astype(BF16)
    ka = section(512, 1024)
    for hh in range(N_HEADS_A):
        put_keys(ka_ref, hh, head_rows(ka, hh))
    va = section(1024, 1536)
    for hh in range(N_HEADS_A):
        put_values(va_ref, hh, head_rows(va, hh))

    qb = section(1536, 2048)
    for hh in range(N_HEADS_B):
        qb_ref[hh // GQA_GROUP, hh % GQA_GROUP] = norm_rope(
            head_rows(qb, hh), gq_ref[...], HEAD_DIM ** -0.5 * LOG2E).astype(BF16)
    kvb = section(2048, 2304)
    for hh in range(N_KV_B):
        put_keys(kb_ref, hh, norm_rope(head_rows(kvb, hh), gk_ref[...], 1.0))
        put_values(vb_ref, hh, head_rows(kvb, N_KV_B + hh))


def _proj(x1, g, w_in_t, cos_t, sin_t, gq, gk):
    b, s, _ = x1.shape
    tm, tk = ATTN_Q_TILE, ATTN_K_TILE
    nq, nc, cpt = s // tm, s // tk, tm // tk
    out_shape = (
        jax.ShapeDtypeStruct((b, N_HEADS_A, nq, HEAD_DIM, tm), BF16),
        jax.ShapeDtypeStruct((b, N_HEADS_A, nc, tk, HEAD_DIM), BF16),
        jax.ShapeDtypeStruct((b, N_HEADS_A, nc, V_ROWS, tk), BF16),
        jax.ShapeDtypeStruct((b, N_KV_B, nq, GQA_GROUP, HEAD_DIM, tm), BF16),
        jax.ShapeDtypeStruct((b, N_KV_B, nc, tk, HEAD_DIM), BF16),
        jax.ShapeDtypeStruct((b, N_KV_B, nc, V_ROWS, tk), BF16),
    )
    out_specs = (
        pl.BlockSpec((None, N_HEADS_A, None, HEAD_DIM, tm), lambda bi, si: (bi, 0, si, 0, 0)),
        pl.BlockSpec((None, N_HEADS_A, cpt, tk, HEAD_DIM), lambda bi, si: (bi, 0, si, 0, 0)),
        pl.BlockSpec((None, N_HEADS_A, cpt, V_ROWS, tk), lambda bi, si: (bi, 0, si, 0, 0)),
        pl.BlockSpec((None, N_KV_B, None, GQA_GROUP, HEAD_DIM, tm), lambda bi, si: (bi, 0, si, 0, 0, 0)),
        pl.BlockSpec((None, N_KV_B, cpt, tk, HEAD_DIM), lambda bi, si: (bi, 0, si, 0, 0)),
        pl.BlockSpec((None, N_KV_B, cpt, V_ROWS, tk), lambda bi, si: (bi, 0, si, 0, 0)),
    )
    return pl.pallas_call(
        _proj_kernel,
        grid=(b, nq),
        in_specs=[
            pl.BlockSpec((None, tm, D_MODEL), lambda bi, si: (bi, si, 0)),
            _resident((1, D_MODEL)),
            _resident((D_IN_PROJ, D_MODEL)),
            pl.BlockSpec((ROPE_AXIS_DIM, tm), lambda bi, si: (0, si)),
            pl.BlockSpec((ROPE_AXIS_DIM, tm), lambda bi, si: (0, si)),
            _resident((HEAD_DIM, 1)),
            _resident((HEAD_DIM, 1)),
        ],
        out_specs=out_specs,
        out_shape=out_shape,
        compiler_params=_params(2),
        name="in_proj",
    )(x1, g, w_in_t, cos_t, sin_t, gq, gk)


def _fold8(x, op):
    rows, cols = x.shape
    return op(x.reshape(rows // 8, 8, cols), axis=0)


def _stream_step(k_ref, vt_ref, s_ref, chunk0, prev_max, new_weights, new_bias_fn):
    nc = k_ref.shape[0]
    m8 = None
    acc = None
    for c in range(nc):
        if prev_max is not None:
            s_prev = s_ref[chunk0 + c]
        if new_weights is not None:
            s = _dot(k_ref[c], new_weights)
            if new_bias_fn is not None:
                s = s + new_bias_fn(c)
            s_ref[chunk0 + c] = s
            cm = _fold8(s, jnp.max)
            m8 = cm if m8 is None else jnp.maximum(m8, cm)
        if prev_max is not None:
            p = jnp.exp2(s_prev - prev_max)
            pv = _dot(vt_ref[c], p.astype(BF16))
            acc = pv if acc is None else acc + pv
    new_max = None if m8 is None else jnp.max(m8, axis=0, keepdims=True)
    return acc, new_max


def _normalized(acc):
    return acc[:HEAD_DIM] / acc[HEAD_DIM:HEAD_DIM + 1]


def _attn_a_kernel(lam_ref, zero_ref, q_ref, k_ref, vt_ref, bias_ref, gsub_ref, o_ref, s_ref):
    nq, _, tq = q_ref.shape
    ratio = tq // ATTN_K_TILE
    chunk0 = zero_ref[0]
    row = lax.broadcasted_iota(jnp.int32, (HEAD_DIM, tq), 0)

    def weights(t, part):
        qt = q_ref[t]
        keep = (row < DIFF_QK_DIM) if part == 0 else (row >= DIFF_QK_DIM)
        return jnp.where(keep, qt, jnp.zeros_like(qt))

    def bias_fn(t):
        return lambda c: bias_ref[jnp.clip(c - t * ratio, -2, ratio + 1) + 2]

    _, m0 = _stream_step(k_ref, vt_ref, s_ref, chunk0, None, weights(0, 0), bias_fn(0))

    def body(t, m_first):
        nxt = jnp.minimum(t + 1, nq - 1)
        acc1, m_second = _stream_step(k_ref, vt_ref, s_ref, chunk0, m_first, weights(t, 1), bias_fn(t))
        acc2, m_next = _stream_step(k_ref, vt_ref, s_ref, chunk0, m_second, weights(nxt, 0), bias_fn(nxt))
        oa = _normalized(acc1) - lam_ref[0, 0] * _normalized(acc2)
        ms = jnp.mean(oa * oa, axis=0, keepdims=True)
        o_ref[t] = (oa * lax.rsqrt(ms + EPS) * gsub_ref[...]).astype(BF16)
        return m_next

    lax.fori_loop(0, nq, body, m0)


def _attn_b_kernel(zero_ref, q_ref, k_ref, vt_ref, o_ref, s_ref):
    nq = q_ref.shape[0]
    chunk0 = zero_ref[0]
    _, m0 = _stream_step(k_ref, vt_ref, s_ref, chunk0, None, q_ref[0, 0], None)

    def body(t, m):
        nxt = jnp.minimum(t + 1, nq - 1)
        for g in range(GQA_GROUP):
            new_w = q_ref[t, g + 1] if g + 1 < GQA_GROUP else q_ref[nxt, 0]
            acc, m = _stream_step(k_ref, vt_ref, s_ref, chunk0, m, new_w, None)
            o_ref[t, g] = _normalized(acc).astype(BF16)
        return m

    lax.fori_loop(0, nq, body, m0)


def _attn_a(lam, q, k, vt, bias, gsub):
    b, h, nq, _, tq = q.shape
    nc, tk = k.shape[2], k.shape[3]
    nt = bias.shape[1]
    return pl.pallas_call(
        _attn_a_kernel,
        grid=(b, h),
        in_specs=[
            pl.BlockSpec(memory_space=pltpu.SMEM),
            pl.BlockSpec(memory_space=pltpu.SMEM),
            pl.BlockSpec((None, None, nq, HEAD_DIM, tq), lambda bi, hi: (bi, hi, 0, 0, 0)),
            pl.BlockSpec((None, None, nc, tk, HEAD_DIM), lambda bi, hi: (bi, hi, 0, 0, 0)),
            pl.BlockSpec((None, None, nc, V_ROWS, tk), lambda bi, hi: (bi, hi, 0, 0, 0)),
            pl.BlockSpec((None, nt, tk, tq), lambda bi, hi: (hi, 0, 0, 0)),
            pl.BlockSpec((HEAD_DIM, 1), lambda bi, hi: (0, 0)),
        ],
        out_specs=pl.BlockSpec((None, None, nq, HEAD_DIM, tq), lambda bi, hi: (bi, hi, 0, 0, 0)),
        out_shape=jax.ShapeDtypeStruct((b, h, nq, HEAD_DIM, tq), BF16),
        scratch_shapes=[pltpu.VMEM((nc, tk, tq), F32)],
        compiler_params=_params(2),
        name="attn_diff",
    )(lam, jnp.zeros((1,), jnp.int32), q, k, vt, bias, gsub)


def _attn_b(q, k, vt):
    b, nkv, nq, grp, _, tq = q.shape
    nc, tk = k.shape[2], k.shape[3]
    return pl.pallas_call(
        _attn_b_kernel,
        grid=(b, nkv),
        in_specs=[
            pl.BlockSpec(memory_space=pltpu.SMEM),
            pl.BlockSpec((None, None, nq, grp, HEAD_DIM, tq), lambda bi, gi: (bi, gi, 0, 0, 0, 0)),
            pl.BlockSpec((None, None, nc, tk, HEAD_DIM), lambda bi, gi: (bi, gi, 0, 0, 0)),
            pl.BlockSpec((None, None, nc, V_ROWS, tk), lambda bi, gi: (bi, gi, 0, 0, 0)),
        ],
        out_specs=pl.BlockSpec((None, None, nq, grp, HEAD_DIM, tq), lambda bi, gi: (bi, gi, 0, 0, 0, 0)),
        out_shape=jax.ShapeDtypeStruct((b, nkv, nq, grp, HEAD_DIM, tq), BF16),
        scratch_shapes=[pltpu.VMEM((nc, tk, tq), F32)],
        compiler_params=_params(2),
        name="attn_gqa",
    )(jnp.zeros((1,), jnp.int32), q, k, vt)


def _tail_kernel(x_ref, oa_ref, ob_ref, wo_ref, g2_ref, wg_ref, wu_ref, wd_ref, gf_ref, o_ref):
    tn = (((0,), (0,)), ((), ()))
    tm = x_ref.shape[0]
    oa = oa_ref[...].reshape(D_MIX_A, tm)
    ob = ob_ref[...].reshape(D_MIX_B, tm)
    x2 = (x_ref[...]
          + lax.dot_general(oa, wo_ref[0:D_MIX_A, :], tn, preferred_element_type=F32)
          + lax.dot_general(ob, wo_ref[D_MIX_A:, :], tn, preferred_element_type=F32))
    h = _rms_rows(x2, g2_ref[...]).astype(BF16)
    y = x2 + 0.5 * _swiglu_acc(h, wg_ref, wu_ref, wd_ref)
    o_ref[...] = _rms_rows(y, gf_ref[...])


def _tail(x1, oa, ob, wo, g2, wg, wu, wd, gf):
    b, s, _ = x1.shape
    tm = ATTN_Q_TILE
    return pl.pallas_call(
        _tail_kernel,
        grid=(b, s // tm),
        in_specs=[
            pl.BlockSpec((None, tm, D_MODEL), lambda bi, si: (bi, si, 0)),
            pl.BlockSpec((None, N_HEADS_A, None, HEAD_DIM, tm), lambda bi, si: (bi, 0, si, 0, 0)),
            pl.BlockSpec((None, N_KV_B, None, GQA_GROUP, HEAD_DIM, tm), lambda bi, si: (bi, 0, si, 0, 0, 0)),
            _resident((D_MODEL, D_MODEL)),
            _resident((1, D_MODEL)),
            _resident((D_MODEL, D_FF)),
            _resident((D_MODEL, D_FF)),
            _resident((D_FF, D_MODEL)),
            _resident((1, D_MODEL)),
        ],
        out_specs=pl.BlockSpec((None, tm, D_MODEL), lambda bi, si: (bi, si, 0)),
        out_shape=jax.ShapeDtypeStruct((b, s, D_MODEL), F32),
        compiler_params=_params(2),
        name="out_proj_ffn2",
    )(x1, oa, ob, wo, g2, wg, wu, wd, gf)


def _t5_bucket(rel):
    nb = N_BUCKETS // 2
    max_exact = nb // 2
    ret = jnp.where(rel > 0, nb, 0)
    n = jnp.abs(rel)
    nf = jnp.maximum(n, 1).astype(jnp.float32)
    large = max_exact + (jnp.log(nf / max_exact) / math.log(MAX_DISTANCE / max_exact)
                         * (nb - max_exact)).astype(jnp.int32)
    large = jnp.minimum(large, nb - 1)
    return ret + jnp.where(n < max_exact, n, large)


def _bias_base_rows(rel_bias, tq, tk):
    nt = tq // tk + 4
    delta = jnp.arange(-2, nt - 2, dtype=jnp.int32)[:, None] * tk
    y = jnp.arange(tq + tk, dtype=jnp.int32)[None, :]
    table = rel_bias.astype(F32) * LOG2E
    base = table[_t5_bucket(delta + tk - y)]
    return jnp.moveaxis(base, -1, 0)[:, :, None, :]


def _rope_tables_t(s):
    rows = s // GRID_W
    row = jnp.repeat(jnp.arange(rows, dtype=jnp.int32), GRID_W).astype(F32)
    col = jnp.tile(jnp.arange(GRID_W, dtype=jnp.int32), rows).astype(F32)
    inv = ROPE_THETA ** (-jnp.arange(0, ROPE_AXIS_DIM, 2, dtype=F32) / ROPE_AXIS_DIM)
    ang = jnp.concatenate([row[:, None] * inv[None], col[:, None] * inv[None]], axis=-1)
    return jnp.cos(ang).T, jnp.sin(ang).T


def kernel(x, ffn1_norm, ffn1_w_gate, ffn1_w_up, ffn1_w_down, mix_norm, w_in, lambda_q1, lambda_k1, lambda_q2, lambda_k2, diff_subln, q_norm, k_norm, rel_bias, w_out, ffn2_norm, ffn2_w_gate, ffn2_w_up, ffn2_w_down, final_norm):
    b, s, d = x.shape
    tq, tk = ATTN_Q_TILE, ATTN_K_TILE
    lyr = 0
    lambda_init = 0.8 - 0.6 * math.exp(-0.3 * lyr)

    deint = np.concatenate([np.arange(0, HEAD_DIM, 2), np.arange(1, HEAD_DIM, 2)])
    col = np.arange(D_IN_PROJ)
    qb_cols = 1536 + (np.arange(N_HEADS_B)[:, None] * HEAD_DIM + deint[None, :]).reshape(-1)
    kb_cols = 2048 + (np.arange(N_KV_B)[:, None] * HEAD_DIM + deint[None, :]).reshape(-1)
    col[1536:2048] = qb_cols
    col[2048:2176] = kb_cols
    w_in_t = w_in[lyr][:, col].T.astype(BF16)
    gq = q_norm[lyr].astype(F32)[deint][:, None]
    gk = k_norm[lyr].astype(F32)[deint][:, None]
    cos_t, sin_t = _rope_tables_t(s)
    bias = _bias_tiles(_bias_base_rows(rel_bias, tq, tk), tq, tk)
    lam = (jnp.exp(jnp.sum(lambda_q1[lyr].astype(F32) * lambda_k1[lyr].astype(F32)))
           - jnp.exp(jnp.sum(lambda_q2[lyr].astype(F32) * lambda_k2[lyr].astype(F32)))
           + lambda_init).reshape(1, 1)
    gsub = (diff_subln[lyr].astype(F32) * (1.0 - lambda_init))[:, None]

    x1 = _ffn(x.reshape(b * s, d), ffn1_norm[lyr][None, :],
              ffn1_w_gate[lyr].astype(BF16), ffn1_w_up[lyr].astype(BF16), ffn1_w_down[lyr].astype(BF16))
    x1 = x1.reshape(b, s, d)

    qa, ka, va, qb, kb, vb = _proj(x1, mix_norm[lyr][None, :], w_in_t, cos_t, sin_t, gq, gk)

    oa = _attn_a(lam, qa, ka, va, bias, gsub)
    ob = _attn_b(qb, kb, vb)

    return _tail(x1, oa, ob, w_out[lyr].astype(BF16), ffn2_norm[lyr][None, :],
                 ffn2_w_gate[lyr].astype(BF16), ffn2_w_up[lyr].astype(BF16), ffn2_w_down[lyr].astype(BF16),
                 final_norm[None, :])
```

```python
import math

import numpy as np
import jax
import jax.numpy as jnp
from jax import lax
from jax.experimental import pallas as pl
from jax.experimental.pallas import tpu as pltpu

D_MODEL = 1024
HEAD_DIM = 64
D_MIX_A = 512
D_MIX_B = 512
N_HEADS_A = 8
DIFF_QK_DIM = 32
N_HEADS_B = 8
GQA_GROUP = 4
N_KV_B = 2
D_FF = 2816
GRID_W = 64
ROPE_THETA = 10000.0
ROPE_AXIS_DIM = 32
N_BUCKETS = 32
MAX_DISTANCE = 128
EPS = 1e-6
D_IN_PROJ = 2304
LOG2E = math.log2(math.e)

VMEM_LIMIT_BYTES = 56 * 1024 * 1024
BF16_SUBLANE_TILE = 16

FFN_TOKEN_TILE = 512
FFN_CHUNKS = ((0, 1024), (1024, 2048), (2048, 2816))
ATTN_Q_TILE = 512
ATTN_K_TILE = 256
ATTN_A_TILES_PER_ITER = 1
V_ROWS = HEAD_DIM + BF16_SUBLANE_TILE

BF16 = jnp.bfloat16
F32 = jnp.float32


def _dot(a, b):
    return jnp.dot(a, b, preferred_element_type=F32)


def _rms_rows(x, g):
    ms = jnp.mean(x * x, axis=-1, keepdims=True)
    return x * lax.rsqrt(ms + EPS) * g


def _swiglu_acc(h, wg_ref, wu_ref, wd_ref):
    y = None
    for s, e in FFN_CHUNKS:
        g = _dot(h, wg_ref[:, s:e])
        u = _dot(h, wu_ref[:, s:e])
        a = (g / (1.0 + jnp.exp(-g)) * u).astype(BF16)
        part = _dot(a, wd_ref[s:e, :])
        y = part if y is None else y + part
    return y


def _resident(shape):
    return pl.BlockSpec(shape, lambda *_: (0,) * len(shape), pipeline_mode=pl.Buffered(1))


def _params(n_grid_axes):
    return pltpu.CompilerParams(dimension_semantics=("arbitrary",) * n_grid_axes,
                                vmem_limit_bytes=VMEM_LIMIT_BYTES)


def _toeplitz_kernel(base_ref, o_ref):
    tk, tq = o_ref.shape
    rows = jnp.broadcast_to(base_ref[...], (tk, tq + tk))
    skew = pltpu.roll(rows, 0, 1, stride=1, stride_axis=0)
    o_ref[...] = skew[:, tk:tk + tq]


def _bias_tiles(base, tq, tk):
    h, nt = base.shape[:2]
    return pl.pallas_call(
        _toeplitz_kernel,
        grid=(h, nt),
        in_specs=[pl.BlockSpec((None, None, 1, tq + tk), lambda hi, di: (hi, di, 0, 0))],
        out_specs=pl.BlockSpec((None, None, tk, tq), lambda hi, di: (hi, di, 0, 0)),
        out_shape=jax.ShapeDtypeStruct((h, nt, tk, tq), F32),
        compiler_params=_params(2),
        name="bias_tiles",
    )(base)


def _ffn_kernel(x_ref, g_ref, wg_ref, wu_ref, wd_ref, o_ref):
    x = x_ref[...]
    h = _rms_rows(x, g_ref[...]).astype(BF16)
    o_ref[...] = x + 0.5 * _swiglu_acc(h, wg_ref, wu_ref, wd_ref)


def _ffn(x2d, g, wg, wu, wd):
    n = x2d.shape[0]
    tm = FFN_TOKEN_TILE
    return pl.pallas_call(
        _ffn_kernel,
        grid=(n // tm,),
        in_specs=[
            pl.BlockSpec((tm, D_MODEL), lambda i: (i, 0)),
            _resident((1, D_MODEL)),
            _resident((D_MODEL, D_FF)),
            _resident((D_MODEL, D_FF)),
            _resident((D_FF, D_MODEL)),
        ],
        out_specs=pl.BlockSpec((tm, D_MODEL), lambda i: (i, 0)),
        out_shape=jax.ShapeDtypeStruct((n, D_MODEL), F32),
        compiler_params=_params(1),
        name="ffn1",
    )(x2d, g, wg, wu, wd)


def _proj_kernel(x_ref, g_ref, wt_ref, cos_ref, sin_ref, gq_ref, gk_ref,
                 qa_ref, ka_ref, va_ref, qb_ref, kb_ref, vb_ref):
    tk = ATTN_K_TILE
    n_chunks = x_ref.shape[0] // tk
    h = _rms_rows(x_ref[...], g_ref[...]).astype(BF16)
    cos = cos_ref[...]
    sin = sin_ref[...]
    ones = jnp.ones((V_ROWS - HEAD_DIM, tk), BF16)

    def section(r0, r1):
        return lax.dot_general(wt_ref[r0:r1, :], h, (((1,), (1,)), ((), ())),
                               preferred_element_type=F32)

    def norm_rope(t, gain, scale):
        ms = jnp.mean(t * t, axis=0, keepdims=True)
        t = t * lax.rsqrt(ms + EPS) * gain
        te, to = t[:32], t[32:]
        re = te * cos - to * sin
        ro = te * sin + to * cos
        return jnp.concatenate([re, ro], axis=0) * scale

    def put_keys(dst_ref, head, kt):
        for j in range(n_chunks):
            dst_ref[head, j] = kt[:, j * tk:(j + 1) * tk].T.astype(BF16)

    def put_values(dst_ref, head, vt):
        for j in range(n_chunks):
            dst_ref[head, j, 0:HEAD_DIM, :] = vt[:, j * tk:(j + 1) * tk].astype(BF16)
            dst_ref[head, j, HEAD_DIM:V_ROWS, :] = ones

    def head_rows(t, head):
        return t[head * HEAD_DIM:(head + 1) * HEAD_DIM]

    qa = section(0, 512) * (DIFF_QK_DIM ** -0.5 * LOG2E)
    for hh in range(N_HEADS_A):
        qa_ref[hh] = head_rows(qa, hh).astype(BF16)
    ka = section(512, 1024)
    for hh in range(N_HEADS_A):
        put_keys(ka_ref, hh, head_rows(ka, hh))
    va = section(1024, 1536)
    for hh in range(N_HEADS_A):
        put_values(va_ref, hh, head_rows(va, hh))

    qb = section(1536, 2048)
    for hh in range(N_HEADS_B):
        qb_ref[hh // GQA_GROUP, hh % GQA_GROUP] = norm_rope(
            head_rows(qb, hh), gq_ref[...], HEAD_DIM ** -0.5 * LOG2E).astype(BF16)
    kvb = section(2048, 2304)
    for hh in range(N_KV_B):
        put_keys(kb_ref, hh, norm_rope(head_rows(kvb, hh), gk_ref[...], 1.0))
        put_values(vb_ref, hh, head_rows(kvb, N_KV_B + hh))


def _proj(x1, g, w_in_t, cos_t, sin_t, gq, gk):
    b, s, _ = x1.shape
    tm, tk = ATTN_Q_TILE, ATTN_K_TILE
    nq, nc, cpt = s // tm, s // tk, tm // tk
    out_shape = (
        jax.ShapeDtypeStruct((b, N_HEADS_A, nq, HEAD_DIM, tm), BF16),
        jax.ShapeDtypeStruct((b, N_HEADS_A, nc, tk, HEAD_DIM), BF16),
        jax.ShapeDtypeStruct((b, N_HEADS_A, nc, V_ROWS, tk), BF16),
        jax.ShapeDtypeStruct((b, N_KV_B, nq, GQA_GROUP, HEAD_DIM, tm), BF16),
        jax.ShapeDtypeStruct((b, N_KV_B, nc, tk, HEAD_DIM), BF16),
        jax.ShapeDtypeStruct((b, N_KV_B, nc, V_ROWS, tk), BF16),
    )
    out_specs = (
        pl.BlockSpec((None, N_HEADS_A, None, HEAD_DIM, tm), lambda bi, si: (bi, 0, si, 0, 0)),
        pl.BlockSpec((None, N_HEADS_A, cpt, tk, HEAD_DIM), lambda bi, si: (bi, 0, si, 0, 0)),
        pl.BlockSpec((None, N_HEADS_A, cpt, V_ROWS, tk), lambda bi, si: (bi, 0, si, 0, 0)),
        pl.BlockSpec((None, N_KV_B, None, GQA_GROUP, HEAD_DIM, tm), lambda bi, si: (bi, 0, si, 0, 0, 0)),
        pl.BlockSpec((None, N_KV_B, cpt, tk, HEAD_DIM), lambda bi, si: (bi, 0, si, 0, 0)),
        pl.BlockSpec((None, N_KV_B, cpt, V_ROWS, tk), lambda bi, si: (bi, 0, si, 0, 0)),
    )
    return pl.pallas_call(
        _proj_kernel,
        grid=(b, nq),
        in_specs=[
            pl.BlockSpec((None, tm, D_MODEL), lambda bi, si: (bi, si, 0)),
            _resident((1, D_MODEL)),
            _resident((D_IN_PROJ, D_MODEL)),
            pl.BlockSpec((ROPE_AXIS_DIM, tm), lambda bi, si: (0, si)),
            pl.BlockSpec((ROPE_AXIS_DIM, tm), lambda bi, si: (0, si)),
            _resident((HEAD_DIM, 1)),
            _resident((HEAD_DIM, 1)),
        ],
        out_specs=out_specs,
        out_shape=out_shape,
        compiler_params=_params(2),
        name="in_proj",
    )(x1, g, w_in_t, cos_t, sin_t, gq, gk)


def _fold8(x, op):
    rows, cols = x.shape
    return op(x.reshape(rows // 8, 8, cols), axis=0)


class _NoBias:
    def chunk(self, j, first):
        return j

    def tile(self, j, first):
        return None

    def shift(self, j, first):
        return None


class _RelativeBias:
    def __init__(self, bias_ref, nc, ratio):
        self.bias_ref, self.nc, self.ratio = bias_ref, nc, ratio
        self.nt = bias_ref.shape[0]
        self.row_before = bias_ref[0, 0:1, :]
        self.row_after = bias_ref[self.nt - 1, 0:1, :]

    def chunk(self, j, first):
        c = first + (j - 1)
        return jnp.where(c < 0, c + self.nc, jnp.where(c >= self.nc, c - self.nc, c))

    def tile(self, j, first):
        if j - 1 > self.ratio:
            return None
        c = first + (j - 1)
        d = jnp.where(c < 0, self.nt - 1, jnp.where(c >= self.nc, 0, j + 1))
        return self.bias_ref[d]

    def shift(self, j, first):
        if j - 1 <= self.ratio:
            return None
        return jnp.where(first + (j - 1) >= self.nc, self.row_before, self.row_after)


def _stream_step(k_ref, vt_ref, s_ref, slot0, plan, prev, new):
    nc = k_ref.shape[0]
    m8 = None
    acc = None
    for j in range(nc):
        if prev is not None:
            prev_max, prev_first = prev
            s_prev = s_ref[slot0 + j]
        if new is not None:
            weights, first = new
            s = _dot(k_ref[plan.chunk(j, first)], weights)
            tile = plan.tile(j, first)
            if tile is not None:
                s = s + tile
            s_ref[slot0 + j] = s
            cm = _fold8(s, jnp.max)
            shift = plan.shift(j, first)
            if shift is not None:
                cm = cm + shift
            m8 = cm if m8 is None else jnp.maximum(m8, cm)
        if prev is not None:
            shift = plan.shift(j, prev_first)
            p = jnp.exp2(s_prev - (prev_max if shift is None else prev_max - shift))
            pv = _dot(vt_ref[plan.chunk(j, prev_first)], p.astype(BF16))
            acc = pv if acc is None else acc + pv
    new_max = None if m8 is None else jnp.max(m8, axis=0, keepdims=True)
    return acc, new_max


def _normalized(acc):
    return acc[:HEAD_DIM] / acc[HEAD_DIM:HEAD_DIM + 1]


def _attn_a_kernel(lam_ref, zero_ref, q_ref, k_ref, vt_ref, bias_ref, gsub_ref, o_ref, s_ref):
    nq, _, tq = q_ref.shape
    nc = k_ref.shape[0]
    ratio = tq // ATTN_K_TILE
    slot0 = zero_ref[0]
    plan = _RelativeBias(bias_ref, nc, ratio)
    row = lax.broadcasted_iota(jnp.int32, (HEAD_DIM, tq), 0)

    def weights(t, part):
        qt = q_ref[t]
        keep = (row < DIFF_QK_DIM) if part == 0 else (row >= DIFF_QK_DIM)
        return jnp.where(keep, qt, jnp.zeros_like(qt))

    def step(prev, new):
        return _stream_step(k_ref, vt_ref, s_ref, slot0, plan, prev, new)

    _, m0 = step(None, (weights(0, 0), 0))

    def body(it, m):
        for u in range(ATTN_A_TILES_PER_ITER):
            t = it * ATTN_A_TILES_PER_ITER + u
            nxt = jnp.minimum(t + 1, nq - 1)
            acc1, m = step((m, t * ratio), (weights(t, 1), t * ratio))
            acc2, m = step((m, t * ratio), (weights(nxt, 0), nxt * ratio))
            oa = _normalized(acc1) - lam_ref[0, 0] * _normalized(acc2)
            ms = jnp.mean(oa * oa, axis=0, keepdims=True)
            o_ref[t] = (oa * lax.rsqrt(ms + EPS) * gsub_ref[...]).astype(BF16)
        return m

    lax.fori_loop(0, nq // ATTN_A_TILES_PER_ITER, body, m0)


def _attn_b_kernel(zero_ref, q_ref, k_ref, vt_ref, o_ref, s_ref):
    nq = q_ref.shape[0]
    slot0 = zero_ref[0]
    plan = _NoBias()

    def step(prev, new):
        return _stream_step(k_ref, vt_ref, s_ref, slot0, plan, prev, new)

    _, m0 = step(None, (q_ref[0, 0], 0))

    def body(t, m):
        nxt = jnp.minimum(t + 1, nq - 1)
        for g in range(GQA_GROUP):
            new_w = q_ref[t, g + 1] if g + 1 < GQA_GROUP else q_ref[nxt, 0]
            acc, m = step((m, 0), (new_w, 0))
            o_ref[t, g] = _normalized(acc).astype(BF16)
        return m

    lax.fori_loop(0, nq, body, m0)


def _attn_a(lam, q, k, vt, bias, gsub):
    b, h, nq, _, tq = q.shape
    nc, tk = k.shape[2], k.shape[3]
    nt = bias.shape[1]
    return pl.pallas_call(
        _attn_a_kernel,
        grid=(b, h),
        in_specs=[
            pl.BlockSpec(memory_space=pltpu.SMEM),
            pl.BlockSpec(memory_space=pltpu.SMEM),
            pl.BlockSpec((None, None, nq, HEAD_DIM, tq), lambda bi, hi: (bi, hi, 0, 0, 0)),
            pl.BlockSpec((None, None, nc, tk, HEAD_DIM), lambda bi, hi: (bi, hi, 0, 0, 0)),
            pl.BlockSpec((None, None, nc, V_ROWS, tk), lambda bi, hi: (bi, hi, 0, 0, 0)),
            pl.BlockSpec((None, nt, tk, tq), lambda bi, hi: (hi, 0, 0, 0)),
            pl.BlockSpec((HEAD_DIM, 1), lambda bi, hi: (0, 0)),
        ],
        out_specs=pl.BlockSpec((None, None, nq, HEAD_DIM, tq), lambda bi, hi: (bi, hi, 0, 0, 0)),
        out_shape=jax.ShapeDtypeStruct((b, h, nq, HEAD_DIM, tq), BF16),
        scratch_shapes=[pltpu.VMEM((nc, tk, tq), F32)],
        compiler_params=_params(2),
        name="attn_diff",
    )(lam, jnp.zeros((1,), jnp.int32), q, k, vt, bias, gsub)


def _attn_b(q, k, vt):
    b, nkv, nq, grp, _, tq = q.shape
    nc, tk = k.shape[2], k.shape[3]
    return pl.pallas_call(
        _attn_b_kernel,
        grid=(b, nkv),
        in_specs=[
            pl.BlockSpec(memory_space=pltpu.SMEM),
            pl.BlockSpec((None, None, nq, grp, HEAD_DIM, tq), lambda bi, gi: (bi, gi, 0, 0, 0, 0)),
            pl.BlockSpec((None, None, nc, tk, HEAD_DIM), lambda bi, gi: (bi, gi, 0, 0, 0)),
            pl.BlockSpec((None, None, nc, V_ROWS, tk), lambda bi, gi: (bi, gi, 0, 0, 0)),
        ],
        out_specs=pl.BlockSpec((None, None, nq, grp, HEAD_DIM, tq), lambda bi, gi: (bi, gi, 0, 0, 0, 0)),
        out_shape=jax.ShapeDtypeStruct((b, nkv, nq, grp, HEAD_DIM, tq), BF16),
        scratch_shapes=[pltpu.VMEM((nc, tk, tq), F32)],
        compiler_params=_params(2),
        name="attn_gqa",
    )(jnp.zeros((1,), jnp.int32), q, k, vt)


def _tail_kernel(x_ref, oa_ref, ob_ref, wo_ref, g2_ref, wg_ref, wu_ref, wd_ref, gf_ref, o_ref):
    tn = (((0,), (0,)), ((), ()))
    tm = x_ref.shape[0]
    oa = oa_ref[...].reshape(D_MIX_A, tm)
    ob = ob_ref[...].reshape(D_MIX_B, tm)
    x2 = (x_ref[...]
          + lax.dot_general(oa, wo_ref[0:D_MIX_A, :], tn, preferred_element_type=F32)
          + lax.dot_general(ob, wo_ref[D_MIX_A:, :], tn, preferred_element_type=F32))
    h = _rms_rows(x2, g2_ref[...]).astype(BF16)
    y = x2 + 0.5 * _swiglu_acc(h, wg_ref, wu_ref, wd_ref)
    o_ref[...] = _rms_rows(y, gf_ref[...])


def _tail(x1, oa, ob, wo, g2, wg, wu, wd, gf):
    b, s, _ = x1.shape
    tm = ATTN_Q_TILE
    return pl.pallas_call(
        _tail_kernel,
        grid=(b, s // tm),
        in_specs=[
            pl.BlockSpec((None, tm, D_MODEL), lambda bi, si: (bi, si, 0)),
            pl.BlockSpec((None, N_HEADS_A, None, HEAD_DIM, tm), lambda bi, si: (bi, 0, si, 0, 0)),
            pl.BlockSpec((None, N_KV_B, None, GQA_GROUP, HEAD_DIM, tm), lambda bi, si: (bi, 0, si, 0, 0, 0)),
            _resident((D_MODEL, D_MODEL)),
            _resident((1, D_MODEL)),
            _resident((D_MODEL, D_FF)),
            _resident((D_MODEL, D_FF)),
            _resident((D_FF, D_MODEL)),
            _resident((1, D_MODEL)),
        ],
        out_specs=pl.BlockSpec((None, tm, D_MODEL), lambda bi, si: (bi, si, 0)),
        out_shape=jax.ShapeDtypeStruct((b, s, D_MODEL), F32),
        compiler_params=_params(2),
        name="out_proj_ffn2",
    )(x1, oa, ob, wo, g2, wg, wu, wd, gf)


def _t5_bucket(rel):
    nb = N_BUCKETS // 2
    max_exact = nb // 2
    ret = jnp.where(rel > 0, nb, 0)
    n = jnp.abs(rel)
    nf = jnp.maximum(n, 1).astype(jnp.float32)
    large = max_exact + (jnp.log(nf / max_exact) / math.log(MAX_DISTANCE / max_exact)
                         * (nb - max_exact)).astype(jnp.int32)
    large = jnp.minimum(large, nb - 1)
    return ret + jnp.where(n < max_exact, n, large)


def _bias_base_rows(rel_bias, tq, tk):
    nt = tq // tk + 4
    delta = jnp.arange(-2, nt - 2, dtype=jnp.int32)[:, None] * tk
    y = jnp.arange(tq + tk, dtype=jnp.int32)[None, :]
    table = rel_bias.astype(F32) * LOG2E
    base = table[_t5_bucket(delta + tk - y)]
    return jnp.moveaxis(base, -1, 0)[:, :, None, :]


def _rope_tables_t(s):
    rows = s // GRID_W
    row = jnp.repeat(jnp.arange(rows, dtype=jnp.int32), GRID_W).astype(F32)
    col = jnp.tile(jnp.arange(GRID_W, dtype=jnp.int32), rows).astype(F32)
    inv = ROPE_THETA ** (-jnp.arange(0, ROPE_AXIS_DIM, 2, dtype=F32) / ROPE_AXIS_DIM)
    ang = jnp.concatenate([row[:, None] * inv[None], col[:, None] * inv[None]], axis=-1)
    return jnp.cos(ang).T, jnp.sin(ang).T


def kernel(x, ffn1_norm, ffn1_w_gate, ffn1_w_up, ffn1_w_down, mix_norm, w_in, lambda_q1, lambda_k1, lambda_q2, lambda_k2, diff_subln, q_norm, k_norm, rel_bias, w_out, ffn2_norm, ffn2_w_gate, ffn2_w_up, ffn2_w_down, final_norm):
    b, s, d = x.shape
    tq, tk = ATTN_Q_TILE, ATTN_K_TILE
    lyr = 0
    lambda_init = 0.8 - 0.6 * math.exp(-0.3 * lyr)

    deint = np.concatenate([np.arange(0, HEAD_DIM, 2), np.arange(1, HEAD_DIM, 2)])
    col = np.arange(D_IN_PROJ)
    qb_cols = 1536 + (np.arange(N_HEADS_B)[:, None] * HEAD_DIM + deint[None, :]).reshape(-1)
    kb_cols = 2048 + (np.arange(N_KV_B)[:, None] * HEAD_DIM + deint[None, :]).reshape(-1)
    col[1536:2048] = qb_cols
    col[2048:2176] = kb_cols
    w_in_t = w_in[lyr][:, col].T.astype(BF16)
    gq = q_norm[lyr].astype(F32)[deint][:, None]
    gk = k_norm[lyr].astype(F32)[deint][:, None]
    cos_t, sin_t = _rope_tables_t(s)
    bias = _bias_tiles(_bias_base_rows(rel_bias, tq, tk), tq, tk)
    lam = (jnp.exp(jnp.sum(lambda_q1[lyr].astype(F32) * lambda_k1[lyr].astype(F32)))
           - jnp.exp(jnp.sum(lambda_q2[lyr].astype(F32) * lambda_k2[lyr].astype(F32)))
           + lambda_init).reshape(1, 1)
    gsub = (diff_subln[lyr].astype(F32) * (1.0 - lambda_init))[:, None]

    x1 = _ffn(x.reshape(b * s, d), ffn1_norm[lyr][None, :],
              ffn1_w_gate[lyr].astype(BF16), ffn1_w_up[lyr].astype(BF16), ffn1_w_down[lyr].astype(BF16))
    x1 = x1.reshape(b, s, d)

    qa, ka, va, qb, kb, vb = _proj(x1, mix_norm[lyr][None, :], w_in_t, cos_t, sin_t, gq, gk)

    oa = _attn_a(lam, qa, ka, va, bias, gsub)
    ob = _attn_b(qb, kb, vb)

    return _tail(x1, oa, ob, w_out[lyr].astype(BF16), ffn2_norm[lyr][None, :],
                 ffn2_w_gate[lyr].astype(BF16), ffn2_w_up[lyr].astype(BF16), ffn2_w_down[lyr].astype(BF16),
                 final_norm[None, :])
```

```python
import math

import jax
import jax.numpy as jnp
from jax import lax
from jax.experimental import pallas as pl
from jax.experimental.pallas import tpu as pltpu

D_MODEL = 1024
HEAD_DIM = 64
D_MIX_A = 512
D_MIX_B = 512
N_HEADS_A = 8
DIFF_QK_DIM = 32
N_HEADS_B = 8
GQA_GROUP = 4
N_KV_B = 2
D_FF = 2816
GRID_W = 64
ROPE_THETA = 10000.0
ROPE_AXIS_DIM = 32
N_BUCKETS = 32
MAX_DISTANCE = 128
EPS = 1e-6
D_IN_PROJ = 2304
LOG2E = math.log2(math.e)

VMEM_LIMIT_BYTES = 56 * 1024 * 1024
BF16_SUBLANE_TILE = 16

FFN_TOKEN_TILE = 512
FFN_CHUNKS = ((0, 1024), (1024, 2048), (2048, 2816))
ATTN_Q_TILE = 512
ATTN_K_TILE = 256
ATTN_A_HEADS_PER_STEP = 2
V_ROWS = HEAD_DIM + BF16_SUBLANE_TILE

BF16 = jnp.bfloat16
F32 = jnp.float32


def _dot(a, b):
    return jnp.dot(a, b, preferred_element_type=F32)


def _rms_rows(x, g):
    ms = jnp.mean(x * x, axis=-1, keepdims=True)
    return x * lax.rsqrt(ms + EPS) * g


def _swiglu_acc(h, wg_ref, wu_ref, wd_ref):
    y = None
    for s, e in FFN_CHUNKS:
        g = _dot(h, wg_ref[:, s:e])
        u = _dot(h, wu_ref[:, s:e])
        a = (g / (1.0 + jnp.exp(-g)) * u).astype(BF16)
        part = _dot(a, wd_ref[s:e, :])
        y = part if y is None else y + part
    return y


def _resident(shape):
    return pl.BlockSpec(shape, lambda *_: (0,) * len(shape), pipeline_mode=pl.Buffered(1))


def _params(n_grid_axes):
    return pltpu.CompilerParams(dimension_semantics=("arbitrary",) * n_grid_axes,
                                vmem_limit_bytes=VMEM_LIMIT_BYTES)


def _toeplitz_kernel(base_ref, o_ref):
    nt, tk, tq = o_ref.shape
    for d in range(nt):
        rows = jnp.broadcast_to(base_ref[d], (tk, tq + tk))
        skew = pltpu.roll(rows, 0, 1, stride=1, stride_axis=0)
        o_ref[d] = skew[:, tk:tk + tq]


def _bias_tiles(base, tq, tk):
    h, nt = base.shape[:2]
    return pl.pallas_call(
        _toeplitz_kernel,
        grid=(h,),
        in_specs=[pl.BlockSpec((None, nt, 1, tq + tk), lambda hi: (hi, 0, 0, 0))],
        out_specs=pl.BlockSpec((None, nt, tk, tq), lambda hi: (hi, 0, 0, 0)),
        out_shape=jax.ShapeDtypeStruct((h, nt, tk, tq), F32),
        compiler_params=_params(1),
        name="bias_tiles",
    )(base)


def _ffn_kernel(x_ref, g_ref, wg_ref, wu_ref, wd_ref, o_ref):
    x = x_ref[...]
    h = _rms_rows(x, g_ref[...]).astype(BF16)
    o_ref[...] = x + 0.5 * _swiglu_acc(h, wg_ref, wu_ref, wd_ref)


def _ffn(x2d, g, wg, wu, wd):
    n = x2d.shape[0]
    tm = FFN_TOKEN_TILE
    return pl.pallas_call(
        _ffn_kernel,
        grid=(n // tm,),
        in_specs=[
            pl.BlockSpec((tm, D_MODEL), lambda i: (i, 0)),
            _resident((1, D_MODEL)),
            _resident((D_MODEL, D_FF)),
            _resident((D_MODEL, D_FF)),
            _resident((D_FF, D_MODEL)),
        ],
        out_specs=pl.BlockSpec((tm, D_MODEL), lambda i: (i, 0)),
        out_shape=jax.ShapeDtypeStruct((n, D_MODEL), F32),
        compiler_params=_params(1),
        name="ffn1",
    )(x2d, g, wg, wu, wd)


def _proj_kernel(x_ref, g_ref, wt_ref, cos_ref, sin_ref, gq_ref, gk_ref,
                 qa_ref, ka_ref, va_ref, qb_ref, kb_ref, vb_ref):
    tk = ATTN_K_TILE
    n_chunks = x_ref.shape[0] // tk
    h = _rms_rows(x_ref[...], g_ref[...]).astype(BF16)
    cos = cos_ref[...]
    sin = sin_ref[...]
    ones = jnp.ones((V_ROWS - HEAD_DIM, tk), BF16)

    def section(r0, r1):
        return lax.dot_general(wt_ref[r0:r1, :], h, (((1,), (1,)), ((), ())),
                               preferred_element_type=F32)

    def norm_rope(t, gain, scale):
        ms = jnp.mean(t * t, axis=0, keepdims=True)
        t = t * lax.rsqrt(ms + EPS) * gain
        te, to = t[:32], t[32:]
        re = te * cos - to * sin
        ro = te * sin + to * cos
        return jnp.concatenate([re, ro], axis=0) * scale

    def put_keys(dst_ref, head, kt):
        for j in range(n_chunks):
            dst_ref[head, j] = kt[:, j * tk:(j + 1) * tk].T.astype(BF16)

    def put_values(dst_ref, head, vt):
        for j in range(n_chunks):
            dst_ref[head, j, 0:HEAD_DIM, :] = vt[:, j * tk:(j + 1) * tk].astype(BF16)
            dst_ref[head, j, HEAD_DIM:V_ROWS, :] = ones

    def head_rows(t, head):
        return t[head * HEAD_DIM:(head + 1) * HEAD_DIM]

    qa = section(0, 512) * (DIFF_QK_DIM ** -0.5 * LOG2E)
    for hh in range(N_HEADS_A):
        qa_ref[hh] = head_rows(qa, hh).astype(BF16)
    ka = section(512, 1024)
    for hh in range(N_HEADS_A):
        put_keys(ka_ref, hh, head_rows(ka, hh))
    va = section(1024, 1536)
    for hh in range(N_HEADS_A):
        put_values(va_ref, hh, head_rows(va, hh))

    qb = section(1536, 2048)
    for hh in range(N_HEADS_B):
        qb_ref[hh // GQA_GROUP, hh % GQA_GROUP] = norm_rope(
            head_rows(qb, hh), gq_ref[...], HEAD_DIM ** -0.5 * LOG2E).astype(BF16)
    kvb = section(2048, 2304)
    for hh in range(N_KV_B):
        put_keys(kb_ref, hh, norm_rope(head_rows(kvb, hh), gk_ref[...], 1.0))
        put_values(vb_ref, hh, head_rows(kvb, N_KV_B + hh))


def _proj(x1, g, w_in_t, cos_t, sin_t, gq, gk):
    b, s, _ = x1.shape
    tm, tk = ATTN_Q_TILE, ATTN_K_TILE
    nq, nc, cpt = s // tm, s // tk, tm // tk
    out_shape = (
        jax.ShapeDtypeStruct((b, N_HEADS_A, nq, HEAD_DIM, tm), BF16),
        jax.ShapeDtypeStruct((b, N_HEADS_A, nc, tk, HEAD_DIM), BF16),
        jax.ShapeDtypeStruct((b, N_HEADS_A, nc, V_ROWS, tk), BF16),
        jax.ShapeDtypeStruct((b, N_KV_B, nq, GQA_GROUP, HEAD_DIM, tm), BF16),
        jax.ShapeDtypeStruct((b, N_KV_B, nc, tk, HEAD_DIM), BF16),
        jax.ShapeDtypeStruct((b, N_KV_B, nc, V_ROWS, tk), BF16),
    )
    out_specs = (
        pl.BlockSpec((None, N_HEADS_A, None, HEAD_DIM, tm), lambda bi, si: (bi, 0, si, 0, 0)),
        pl.BlockSpec((None, N_HEADS_A, cpt, tk, HEAD_DIM), lambda bi, si: (bi, 0, si, 0, 0)),
        pl.BlockSpec((None, N_HEADS_A, cpt, V_ROWS, tk), lambda bi, si: (bi, 0, si, 0, 0)),
        pl.BlockSpec((None, N_KV_B, None, GQA_GROUP, HEAD_DIM, tm), lambda bi, si: (bi, 0, si, 0, 0, 0)),
        pl.BlockSpec((None, N_KV_B, cpt, tk, HEAD_DIM), lambda bi, si: (bi, 0, si, 0, 0)),
        pl.BlockSpec((None, N_KV_B, cpt, V_ROWS, tk), lambda bi, si: (bi, 0, si, 0, 0)),
    )
    return pl.pallas_call(
        _proj_kernel,
        grid=(b, nq),
        in_specs=[
            pl.BlockSpec((None, tm, D_MODEL), lambda bi, si: (bi, si, 0)),
            _resident((1, D_MODEL)),
            _resident((D_IN_PROJ, D_MODEL)),
            pl.BlockSpec((ROPE_AXIS_DIM, tm), lambda bi, si: (0, si)),
            pl.BlockSpec((ROPE_AXIS_DIM, tm), lambda bi, si: (0, si)),
            _resident((HEAD_DIM, 1)),
            _resident((HEAD_DIM, 1)),
        ],
        out_specs=out_specs,
        out_shape=out_shape,
        compiler_params=_params(2),
        name="in_proj",
    )(x1, g, w_in_t, cos_t, sin_t, gq, gk)


def _fold8(x, op):
    rows, cols = x.shape
    return op(x.reshape(rows // 8, 8, cols), axis=0)


class _NoBias:
    def chunk(self, j, first):
        return j

    def tile(self, j, first, head):
        return None

    def shift(self, j, first, head):
        return None


class _RelativeBias:
    def __init__(self, bias_ref, nc, ratio):
        self.bias_ref, self.nc, self.ratio = bias_ref, nc, ratio
        self.nt = bias_ref.shape[1]

    def chunk(self, j, first):
        c = first + (j - 1)
        return jnp.where(c < 0, c + self.nc, jnp.where(c >= self.nc, c - self.nc, c))

    def tile(self, j, first, head):
        if j - 1 > self.ratio:
            return None
        c = first + (j - 1)
        d = jnp.where(c < 0, self.nt - 1, jnp.where(c >= self.nc, 0, j + 1))
        return self.bias_ref[head, d]

    def shift(self, j, first, head):
        if j - 1 <= self.ratio:
            return None
        row_before = self.bias_ref[head, 0, 0:1, :]
        row_after = self.bias_ref[head, self.nt - 1, 0:1, :]
        return jnp.where(first + (j - 1) >= self.nc, row_before, row_after)


def _stream_step(k_ref, vt_ref, s_ref, slot0, plan, prev, new):
    nc = k_ref.shape[1]
    m8 = None
    acc = None
    for j in range(nc):
        if prev is not None:
            prev_max, prev_first, prev_head = prev
            s_prev = s_ref[slot0 + j]
        if new is not None:
            weights, first, head = new
            s = _dot(k_ref[head, plan.chunk(j, first)], weights)
            tile = plan.tile(j, first, head)
            if tile is not None:
                s = s + tile
            s_ref[slot0 + j] = s
            cm = _fold8(s, jnp.max)
            shift = plan.shift(j, first, head)
            if shift is not None:
                cm = cm + shift
            m8 = cm if m8 is None else jnp.maximum(m8, cm)
        if prev is not None:
            shift = plan.shift(j, prev_first, prev_head)
            p = jnp.exp2(s_prev - (prev_max if shift is None else prev_max - shift))
            pv = _dot(vt_ref[prev_head, plan.chunk(j, prev_first)], p.astype(BF16))
            acc = pv if acc is None else acc + pv
    new_max = None if m8 is None else jnp.max(m8, axis=0, keepdims=True)
    return acc, new_max


def _normalized(acc):
    return acc[:HEAD_DIM] / acc[HEAD_DIM:HEAD_DIM + 1]


def _split_index(it, inner):
    assert inner & (inner - 1) == 0
    return lax.shift_right_logical(it, inner.bit_length() - 1), it & (inner - 1)


def _attn_a_kernel(lam_ref, zero_ref, q_ref, k_ref, vt_ref, bias_ref, gsub_ref, o_ref, s_ref):
    heads, nq, _, tq = q_ref.shape
    nc = k_ref.shape[1]
    ratio = tq // ATTN_K_TILE
    slot0 = zero_ref[0]
    plan = _RelativeBias(bias_ref, nc, ratio)
    row = lax.broadcasted_iota(jnp.int32, (HEAD_DIM, tq), 0)
    n_tiles = heads * nq

    def weights(head, t, part):
        qt = q_ref[head, t]
        keep = (row < DIFF_QK_DIM) if part == 0 else (row >= DIFF_QK_DIM)
        return jnp.where(keep, qt, jnp.zeros_like(qt))

    def step(prev, new):
        return _stream_step(k_ref, vt_ref, s_ref, slot0, plan, prev, new)

    _, m0 = step(None, (weights(0, 0, 0), 0, 0))

    def body(it, m):
        head, t = _split_index(it, nq)
        head_n, t_n = _split_index(jnp.minimum(it + 1, n_tiles - 1), nq)
        acc1, m = step((m, t * ratio, head), (weights(head, t, 1), t * ratio, head))
        acc2, m = step((m, t * ratio, head), (weights(head_n, t_n, 0), t_n * ratio, head_n))
        oa = _normalized(acc1) - lam_ref[0, 0] * _normalized(acc2)
        ms = jnp.mean(oa * oa, axis=0, keepdims=True)
        o_ref[head, t] = (oa * lax.rsqrt(ms + EPS) * gsub_ref[...]).astype(BF16)
        return m

    lax.fori_loop(0, n_tiles, body, m0)


def _attn_b_kernel(zero_ref, q_ref, k_ref, vt_ref, o_ref, s_ref):
    heads, nq = q_ref.shape[:2]
    slot0 = zero_ref[0]
    plan = _NoBias()
    n_tiles = heads * nq

    def step(prev, new):
        return _stream_step(k_ref, vt_ref, s_ref, slot0, plan, prev, new)

    _, m0 = step(None, (q_ref[0, 0, 0], 0, 0))

    def body(it, m):
        head, t = _split_index(it, nq)
        head_n, t_n = _split_index(jnp.minimum(it + 1, n_tiles - 1), nq)
        for g in range(GQA_GROUP):
            if g + 1 < GQA_GROUP:
                new = (q_ref[head, t, g + 1], 0, head)
            else:
                new = (q_ref[head_n, t_n, 0], 0, head_n)
            acc, m = step((m, 0, head), new)
            o_ref[head, t, g] = _normalized(acc).astype(BF16)
        return m

    lax.fori_loop(0, n_tiles, body, m0)


def _attn_a(lam, q, k, vt, bias, gsub):
    b, h, nq, _, tq = q.shape
    nc, tk = k.shape[2], k.shape[3]
    nt = bias.shape[1]
    hps = ATTN_A_HEADS_PER_STEP
    return pl.pallas_call(
        _attn_a_kernel,
        grid=(b, h // hps),
        in_specs=[
            pl.BlockSpec(memory_space=pltpu.SMEM),
            pl.BlockSpec(memory_space=pltpu.SMEM),
            pl.BlockSpec((None, hps, nq, HEAD_DIM, tq), lambda bi, hi: (bi, hi, 0, 0, 0)),
            pl.BlockSpec((None, hps, nc, tk, HEAD_DIM), lambda bi, hi: (bi, hi, 0, 0, 0)),
            pl.BlockSpec((None, hps, nc, V_ROWS, tk), lambda bi, hi: (bi, hi, 0, 0, 0)),
            pl.BlockSpec((hps, nt, tk, tq), lambda bi, hi: (hi, 0, 0, 0)),
            pl.BlockSpec((HEAD_DIM, 1), lambda bi, hi: (0, 0)),
        ],
        out_specs=pl.BlockSpec((None, hps, nq, HEAD_DIM, tq), lambda bi, hi: (bi, hi, 0, 0, 0)),
        out_shape=jax.ShapeDtypeStruct((b, h, nq, HEAD_DIM, tq), BF16),
        scratch_shapes=[pltpu.VMEM((nc, tk, tq), F32)],
        compiler_params=_params(2),
        name="attn_diff",
    )(lam, jnp.zeros((1,), jnp.int32), q, k, vt, bias, gsub)


def _attn_b(q, k, vt):
    b, nkv, nq, grp, _, tq = q.shape
    nc, tk = k.shape[2], k.shape[3]
    return pl.pallas_call(
        _attn_b_kernel,
        grid=(b,),
        in_specs=[
            pl.BlockSpec(memory_space=pltpu.SMEM),
            pl.BlockSpec((None, nkv, nq, grp, HEAD_DIM, tq), lambda bi: (bi, 0, 0, 0, 0, 0)),
            pl.BlockSpec((None, nkv, nc, tk, HEAD_DIM), lambda bi: (bi, 0, 0, 0, 0)),
            pl.BlockSpec((None, nkv, nc, V_ROWS, tk), lambda bi: (bi, 0, 0, 0, 0)),
        ],
        out_specs=pl.BlockSpec((None, nkv, nq, grp, HEAD_DIM, tq), lambda bi: (bi, 0, 0, 0, 0, 0)),
        out_shape=jax.ShapeDtypeStruct((b, nkv, nq, grp, HEAD_DIM, tq), BF16),
        scratch_shapes=[pltpu.VMEM((nc, tk, tq), F32)],
        compiler_params=_params(1),
        name="attn_gqa",
    )(jnp.zeros((1,), jnp.int32), q, k, vt)


def _tail_kernel(x_ref, oa_ref, ob_ref, wo_ref, g2_ref, wg_ref, wu_ref, wd_ref, gf_ref, o_ref):
    tn = (((0,), (0,)), ((), ()))
    tm = x_ref.shape[0]
    oa = oa_ref[...].reshape(D_MIX_A, tm)
    ob = ob_ref[...].reshape(D_MIX_B, tm)
    x2 = (x_ref[...]
          + lax.dot_general(oa, wo_ref[0:D_MIX_A, :], tn, preferred_element_type=F32)
          + lax.dot_general(ob, wo_ref[D_MIX_A:, :], tn, preferred_element_type=F32))
    h = _rms_rows(x2, g2_ref[...]).astype(BF16)
    y = x2 + 0.5 * _swiglu_acc(h, wg_ref, wu_ref, wd_ref)
    o_ref[...] = _rms_rows(y, gf_ref[...])


def _tail(x1, oa, ob, wo, g2, wg, wu, wd, gf):
    b, s, _ = x1.shape
    tm = ATTN_Q_TILE
    return pl.pallas_call(
        _tail_kernel,
        grid=(b, s // tm),
        in_specs=[
            pl.BlockSpec((None, tm, D_MODEL), lambda bi, si: (bi, si, 0)),
            pl.BlockSpec((None, N_HEADS_A, None, HEAD_DIM, tm), lambda bi, si: (bi, 0, si, 0, 0)),
            pl.BlockSpec((None, N_KV_B, None, GQA_GROUP, HEAD_DIM, tm), lambda bi, si: (bi, 0, si, 0, 0, 0)),
            _resident((D_MODEL, D_MODEL)),
            _resident((1, D_MODEL)),
            _resident((D_MODEL, D_FF)),
            _resident((D_MODEL, D_FF)),
            _resident((D_FF, D_MODEL)),
            _resident((1, D_MODEL)),
        ],
        out_specs=pl.BlockSpec((None, tm, D_MODEL), lambda bi, si: (bi, si, 0)),
        out_shape=jax.ShapeDtypeStruct((b, s, D_MODEL), F32),
        compiler_params=_params(2),
        name="out_proj_ffn2",
    )(x1, oa, ob, wo, g2, wg, wu, wd, gf)


def _t5_bucket(rel):
    nb = N_BUCKETS // 2
    max_exact = nb // 2
    ret = jnp.where(rel > 0, nb, 0)
    n = jnp.abs(rel)
    nf = jnp.maximum(n, 1).astype(jnp.float32)
    large = max_exact + (jnp.log(nf / max_exact) / math.log(MAX_DISTANCE / max_exact)
                         * (nb - max_exact)).astype(jnp.int32)
    large = jnp.minimum(large, nb - 1)
    return ret + jnp.where(n < max_exact, n, large)


def _bias_base_rows(rel_bias, tq, tk):
    nt = tq // tk + 4
    delta = jnp.arange(-2, nt - 2, dtype=jnp.int32)[:, None] * tk
    y = jnp.arange(tq + tk, dtype=jnp.int32)[None, :]
    table = (rel_bias.astype(F32) * LOG2E).T
    bucket = _t5_bucket(delta + tk - y)
    hit = bucket[None, :, :, None] == jnp.arange(N_BUCKETS, dtype=jnp.int32)
    base = jnp.sum(jnp.where(hit, table[:, None, None, :], 0.0), axis=-1)
    return base[:, :, None, :]


def _rope_tables_t(s):
    rows = s // GRID_W
    row = jnp.repeat(jnp.arange(rows, dtype=jnp.int32), GRID_W).astype(F32)
    col = jnp.tile(jnp.arange(GRID_W, dtype=jnp.int32), rows).astype(F32)
    inv = ROPE_THETA ** (-jnp.arange(0, ROPE_AXIS_DIM, 2, dtype=F32) / ROPE_AXIS_DIM)
    ang = jnp.concatenate([row[:, None] * inv[None], col[:, None] * inv[None]], axis=-1)
    return jnp.cos(ang).T, jnp.sin(ang).T


def kernel(x, ffn1_norm, ffn1_w_gate, ffn1_w_up, ffn1_w_down, mix_norm, w_in, lambda_q1, lambda_k1, lambda_q2, lambda_k2, diff_subln, q_norm, k_norm, rel_bias, w_out, ffn2_norm, ffn2_w_gate, ffn2_w_up, ffn2_w_down, final_norm):
    b, s, d = x.shape
    tq, tk = ATTN_Q_TILE, ATTN_K_TILE
    lyr = 0
    lambda_init = 0.8 - 0.6 * math.exp(-0.3 * lyr)

    def deinterleave(t, n_heads):
        lead = t.shape[:-1]
        t = t.reshape(lead + (n_heads, HEAD_DIM // 2, 2))
        return jnp.swapaxes(t, -1, -2).reshape(lead + (n_heads * HEAD_DIM,))

    w = w_in[lyr]
    w_in_t = jnp.concatenate(
        [w[:, :1536], deinterleave(w[:, 1536:2048], N_HEADS_B), deinterleave(w[:, 2048:2176], N_KV_B),
         w[:, 2176:]], axis=1).T.astype(BF16)
    gq = deinterleave(q_norm[lyr].astype(F32), 1)[:, None]
    gk = deinterleave(k_norm[lyr].astype(F32), 1)[:, None]
    cos_t, sin_t = _rope_tables_t(s)
    bias = _bias_tiles(_bias_base_rows(rel_bias, tq, tk), tq, tk)
    lam = (jnp.exp(jnp.sum(lambda_q1[lyr].astype(F32) * lambda_k1[lyr].astype(F32)))
           - jnp.exp(jnp.sum(lambda_q2[lyr].astype(F32) * lambda_k2[lyr].astype(F32)))
           + lambda_init).reshape(1, 1)
    gsub = (diff_subln[lyr].astype(F32) * (1.0 - lambda_init))[:, None]

    x1 = _ffn(x.reshape(b * s, d), ffn1_norm[lyr][None, :],
              ffn1_w_gate[lyr].astype(BF16), ffn1_w_up[lyr].astype(BF16), ffn1_w_down[lyr].astype(BF16))
    x1 = x1.reshape(b, s, d)

    qa, ka, va, qb, kb, vb = _proj(x1, mix_norm[lyr][None, :], w_in_t, cos_t, sin_t, gq, gk)

    oa = _attn_a(lam, qa, ka, va, bias, gsub)
    ob = _attn_b(qb, kb, vb)

    return _tail(x1, oa, ob, w_out[lyr].astype(BF16), ffn2_norm[lyr][None, :],
                 ffn2_w_gate[lyr].astype(BF16), ffn2_w_up[lyr].astype(BF16), ffn2_w_down[lyr].astype(BF16),
                 final_norm[None, :])
```

```python
import math

import jax
import jax.numpy as jnp
from jax import lax
from jax.experimental import pallas as pl
from jax.experimental.pallas import tpu as pltpu

D_MODEL = 1024
HEAD_DIM = 64
D_MIX_A = 512
D_MIX_B = 512
N_HEADS_A = 8
DIFF_QK_DIM = 32
N_HEADS_B = 8
GQA_GROUP = 4
N_KV_B = 2
D_FF = 2816
GRID_W = 64
ROPE_THETA = 10000.0
ROPE_AXIS_DIM = 32
N_BUCKETS = 32
MAX_DISTANCE = 128
EPS = 1e-6
D_IN_PROJ = 2304
LOG2E = math.log2(math.e)

VMEM_LIMIT_BYTES = 56 * 1024 * 1024
BF16_SUBLANE_TILE = 16

FFN_TOKEN_TILE = 512
FFN_CHUNKS = ((0, 1024), (1024, 2048), (2048, 2816))
ATTN_Q_TILE = 512
ATTN_K_TILE = 256
ATTN_A_TILES_PER_ITER = 2
ATTN_A_HEADS_PER_STEP = 4
V_ROWS = HEAD_DIM + BF16_SUBLANE_TILE

BF16 = jnp.bfloat16
F32 = jnp.float32


def _dot(a, b):
    return jnp.dot(a, b, preferred_element_type=F32)


def _rms_rows(x, g):
    ms = jnp.mean(x * x, axis=-1, keepdims=True)
    return x * lax.rsqrt(ms + EPS) * g


def _swiglu_acc(h, wg_ref, wu_ref, wd_ref):
    y = None
    for s, e in FFN_CHUNKS:
        g = _dot(h, wg_ref[:, s:e])
        u = _dot(h, wu_ref[:, s:e])
        a = (g / (1.0 + jnp.exp(-g)) * u).astype(BF16)
        part = _dot(a, wd_ref[s:e, :])
        y = part if y is None else y + part
    return y


def _resident(shape):
    return pl.BlockSpec(shape, lambda *_: (0,) * len(shape), pipeline_mode=pl.Buffered(1))


def _params(n_grid_axes):
    return pltpu.CompilerParams(dimension_semantics=("arbitrary",) * n_grid_axes,
                                vmem_limit_bytes=VMEM_LIMIT_BYTES)


def _toeplitz_kernel(base_ref, o_ref):
    nt, tk, tq = o_ref.shape
    for d in range(nt):
        rows = jnp.broadcast_to(base_ref[d], (tk, tq + tk))
        skew = pltpu.roll(rows, 0, 1, stride=1, stride_axis=0)
        o_ref[d] = skew[:, tk:tk + tq]


def _bias_tiles(base, tq, tk):
    h, nt = base.shape[:2]
    return pl.pallas_call(
        _toeplitz_kernel,
        grid=(h,),
        in_specs=[pl.BlockSpec((None, nt, 1, tq + tk), lambda hi: (hi, 0, 0, 0))],
        out_specs=pl.BlockSpec((None, nt, tk, tq), lambda hi: (hi, 0, 0, 0)),
        out_shape=jax.ShapeDtypeStruct((h, nt, tk, tq), F32),
        compiler_params=_params(1),
        name="bias_tiles",
    )(base)


def _ffn_kernel(x_ref, g_ref, wg_ref, wu_ref, wd_ref, o_ref):
    x = x_ref[...]
    h = _rms_rows(x, g_ref[...]).astype(BF16)
    o_ref[...] = x + 0.5 * _swiglu_acc(h, wg_ref, wu_ref, wd_ref)


def _ffn(x2d, g, wg, wu, wd):
    n = x2d.shape[0]
    tm = FFN_TOKEN_TILE
    return pl.pallas_call(
        _ffn_kernel,
        grid=(n // tm,),
        in_specs=[
            pl.BlockSpec((tm, D_MODEL), lambda i: (i, 0)),
            _resident((1, D_MODEL)),
            _resident((D_MODEL, D_FF)),
            _resident((D_MODEL, D_FF)),
            _resident((D_FF, D_MODEL)),
        ],
        out_specs=pl.BlockSpec((tm, D_MODEL), lambda i: (i, 0)),
        out_shape=jax.ShapeDtypeStruct((n, D_MODEL), F32),
        compiler_params=_params(1),
        name="ffn1",
    )(x2d, g, wg, wu, wd)


def _proj_kernel(x_ref, g_ref, wt_ref, cos_ref, sin_ref, gq_ref, gk_ref,
                 qa_ref, ka_ref, va_ref, qb_ref, kb_ref, vb_ref):
    tk = ATTN_K_TILE
    n_chunks = x_ref.shape[0] // tk
    h = _rms_rows(x_ref[...], g_ref[...]).astype(BF16)
    cos = cos_ref[...]
    sin = sin_ref[...]
    ones = jnp.ones((V_ROWS - HEAD_DIM, tk), BF16)

    def section(r0, r1):
        return lax.dot_general(wt_ref[r0:r1, :], h, (((1,), (1,)), ((), ())),
                               preferred_element_type=F32)

    def norm_rope(t, gain, scale):
        ms = jnp.mean(t * t, axis=0, keepdims=True)
        t = t * lax.rsqrt(ms + EPS) * gain
        te, to = t[:32], t[32:]
        re = te * cos - to * sin
        ro = te * sin + to * cos
        return jnp.concatenate([re, ro], axis=0) * scale

    def put_keys(dst_ref, head, kt):
        for j in range(n_chunks):
            dst_ref[head, j] = kt[:, j * tk:(j + 1) * tk].T.astype(BF16)

    def put_values(dst_ref, head, vt):
        for j in range(n_chunks):
            dst_ref[head, j, 0:HEAD_DIM, :] = vt[:, j * tk:(j + 1) * tk].astype(BF16)
            dst_ref[head, j, HEAD_DIM:V_ROWS, :] = ones

    def head_rows(t, head):
        return t[head * HEAD_DIM:(head + 1) * HEAD_DIM]

    qa = section(0, 512) * (DIFF_QK_DIM ** -0.5 * LOG2E)
    for hh in range(N_HEADS_A):
        qa_ref[hh] = head_rows(qa, hh).astype(BF16)
    ka = section(512, 1024)
    for hh in range(N_HEADS_A):
        put_keys(ka_ref, hh, head_rows(ka, hh))
    va = section(1024, 1536)
    for hh in range(N_HEADS_A):
        put_values(va_ref, hh, head_rows(va, hh))

    qb = section(1536, 2048)
    for hh in range(N_HEADS_B):
        qb_ref[hh // GQA_GROUP, hh % GQA_GROUP] = norm_rope(
            head_rows(qb, hh), gq_ref[...], HEAD_DIM ** -0.5 * LOG2E).astype(BF16)
    kvb = section(2048, 2304)
    for hh in range(N_KV_B):
        put_keys(kb_ref, hh, norm_rope(head_rows(kvb, hh), gk_ref[...], 1.0))
        put_values(vb_ref, hh, head_rows(kvb, N_KV_B + hh))


def _proj(x1, g, w_in_t, cos_t, sin_t, gq, gk):
    b, s, _ = x1.shape
    tm, tk = ATTN_Q_TILE, ATTN_K_TILE
    nq, nc, cpt = s // tm, s // tk, tm // tk
    out_shape = (
        jax.ShapeDtypeStruct((b, N_HEADS_A, nq, HEAD_DIM, tm), BF16),
        jax.ShapeDtypeStruct((b, N_HEADS_A, nc, tk, HEAD_DIM), BF16),
        jax.ShapeDtypeStruct((b, N_HEADS_A, nc, V_ROWS, tk), BF16),
        jax.ShapeDtypeStruct((b, N_KV_B, nq, GQA_GROUP, HEAD_DIM, tm), BF16),
        jax.ShapeDtypeStruct((b, N_KV_B, nc, tk, HEAD_DIM), BF16),
        jax.ShapeDtypeStruct((b, N_KV_B, nc, V_ROWS, tk), BF16),
    )
    out_specs = (
        pl.BlockSpec((None, N_HEADS_A, None, HEAD_DIM, tm), lambda bi, si: (bi, 0, si, 0, 0)),
        pl.BlockSpec((None, N_HEADS_A, cpt, tk, HEAD_DIM), lambda bi, si: (bi, 0, si, 0, 0)),
        pl.BlockSpec((None, N_HEADS_A, cpt, V_ROWS, tk), lambda bi, si: (bi, 0, si, 0, 0)),
        pl.BlockSpec((None, N_KV_B, None, GQA_GROUP, HEAD_DIM, tm), lambda bi, si: (bi, 0, si, 0, 0, 0)),
        pl.BlockSpec((None, N_KV_B, cpt, tk, HEAD_DIM), lambda bi, si: (bi, 0, si, 0, 0)),
        pl.BlockSpec((None, N_KV_B, cpt, V_ROWS, tk), lambda bi, si: (bi, 0, si, 0, 0)),
    )
    return pl.pallas_call(
        _proj_kernel,
        grid=(b, nq),
        in_specs=[
            pl.BlockSpec((None, tm, D_MODEL), lambda bi, si: (bi, si, 0)),
            _resident((1, D_MODEL)),
            _resident((D_IN_PROJ, D_MODEL)),
            pl.BlockSpec((ROPE_AXIS_DIM, tm), lambda bi, si: (0, si)),
            pl.BlockSpec((ROPE_AXIS_DIM, tm), lambda bi, si: (0, si)),
            _resident((HEAD_DIM, 1)),
            _resident((HEAD_DIM, 1)),
        ],
        out_specs=out_specs,
        out_shape=out_shape,
        compiler_params=_params(2),
        name="in_proj",
    )(x1, g, w_in_t, cos_t, sin_t, gq, gk)


def _fold8(x, op):
    rows, cols = x.shape
    return op(x.reshape(rows // 8, 8, cols), axis=0)


class _NoBias:
    def chunk(self, j, first):
        return j

    def tile(self, j, first, head):
        return None

    def shift(self, j, first, head):
        return None


class _RelativeBias:
    def __init__(self, bias_ref, nc, ratio):
        self.bias_ref, self.nc, self.ratio = bias_ref, nc, ratio
        self.nt = bias_ref.shape[1]

    def chunk(self, j, first):
        c = first + (j - 1)
        return jnp.where(c < 0, c + self.nc, jnp.where(c >= self.nc, c - self.nc, c))

    def tile(self, j, first, head):
        if j - 1 > self.ratio:
            return None
        c = first + (j - 1)
        d = jnp.where(c < 0, self.nt - 1, jnp.where(c >= self.nc, 0, j + 1))
        return self.bias_ref[head, d]

    def shift(self, j, first, head):
        if j - 1 <= self.ratio:
            return None
        row_before = self.bias_ref[head, 0, 0:1, :]
        row_after = self.bias_ref[head, self.nt - 1, 0:1, :]
        return jnp.where(first + (j - 1) >= self.nc, row_before, row_after)


def _stream_step(k_ref, vt_ref, s_ref, slot0, plan, prev, new):
    nc = k_ref.shape[1]
    m8 = None
    acc = None
    for j in range(nc):
        if prev is not None:
            prev_max, prev_first, prev_head = prev
            s_prev = s_ref[slot0 + j]
        if new is not None:
            weights, first, head = new
            s = _dot(k_ref[head, plan.chunk(j, first)], weights)
            tile = plan.tile(j, first, head)
            if tile is not None:
                s = s + tile
            s_ref[slot0 + j] = s
            cm = _fold8(s, jnp.max)
            shift = plan.shift(j, first, head)
            if shift is not None:
                cm = cm + shift
            m8 = cm if m8 is None else jnp.maximum(m8, cm)
        if prev is not None:
            shift = plan.shift(j, prev_first, prev_head)
            p = jnp.exp2(s_prev - (prev_max if shift is None else prev_max - shift))
            pv = _dot(vt_ref[prev_head, plan.chunk(j, prev_first)], p.astype(BF16))
            acc = pv if acc is None else acc + pv
    new_max = None if m8 is None else jnp.max(m8, axis=0, keepdims=True)
    return acc, new_max


def _normalized(acc):
    return acc[:HEAD_DIM] / acc[HEAD_DIM:HEAD_DIM + 1]


def _split_index(it, inner):
    assert inner & (inner - 1) == 0
    return lax.shift_right_logical(it, inner.bit_length() - 1), it & (inner - 1)


def _attn_a_kernel(lam_ref, zero_ref, q_ref, k_ref, vt_ref, bias_ref, gsub_ref, o_ref, s_ref):
    heads, nq, _, tq = q_ref.shape
    nc = k_ref.shape[1]
    ratio = tq // ATTN_K_TILE
    slot0 = zero_ref[0]
    plan = _RelativeBias(bias_ref, nc, ratio)
    row = lax.broadcasted_iota(jnp.int32, (HEAD_DIM, tq), 0)
    n_tiles = heads * nq

    def weights(head, t, part):
        qt = q_ref[head, t]
        keep = (row < DIFF_QK_DIM) if part == 0 else (row >= DIFF_QK_DIM)
        return jnp.where(keep, qt, jnp.zeros_like(qt))

    def step(prev, new):
        return _stream_step(k_ref, vt_ref, s_ref, slot0, plan, prev, new)

    _, m0 = step(None, (weights(0, 0, 0), 0, 0))

    def body(i, m):
        for u in range(ATTN_A_TILES_PER_ITER):
            it = i * ATTN_A_TILES_PER_ITER + u
            head, t = _split_index(it, nq)
            head_n, t_n = _split_index(jnp.minimum(it + 1, n_tiles - 1), nq)
            acc1, m = step((m, t * ratio, head), (weights(head, t, 1), t * ratio, head))
            acc2, m = step((m, t * ratio, head), (weights(head_n, t_n, 0), t_n * ratio, head_n))
            oa = _normalized(acc1) - lam_ref[0, 0] * _normalized(acc2)
            ms = jnp.mean(oa * oa, axis=0, keepdims=True)
            o_ref[head, t] = (oa * lax.rsqrt(ms + EPS) * gsub_ref[...]).astype(BF16)
        return m

    lax.fori_loop(0, n_tiles // ATTN_A_TILES_PER_ITER, body, m0)


def _attn_b_kernel(zero_ref, q_ref, k_ref, vt_ref, o_ref, s_ref):
    heads, nq = q_ref.shape[:2]
    slot0 = zero_ref[0]
    plan = _NoBias()
    n_tiles = heads * nq

    def step(prev, new):
        return _stream_step(k_ref, vt_ref, s_ref, slot0, plan, prev, new)

    _, m0 = step(None, (q_ref[0, 0, 0], 0, 0))

    def body(it, m):
        head, t = _split_index(it, nq)
        head_n, t_n = _split_index(jnp.minimum(it + 1, n_tiles - 1), nq)
        for g in range(GQA_GROUP):
            if g + 1 < GQA_GROUP:
                new = (q_ref[head, t, g + 1], 0, head)
            else:
                new = (q_ref[head_n, t_n, 0], 0, head_n)
            acc, m = step((m, 0, head), new)
            o_ref[head, t, g] = _normalized(acc).astype(BF16)
        return m

    lax.fori_loop(0, n_tiles, body, m0)


def _attn_a(lam, q, k, vt, bias, gsub):
    b, h, nq, _, tq = q.shape
    nc, tk = k.shape[2], k.shape[3]
    nt = bias.shape[1]
    hps = ATTN_A_HEADS_PER_STEP
    return pl.pallas_call(
        _attn_a_kernel,
        grid=(b, h // hps),
        in_specs=[
            pl.BlockSpec(memory_space=pltpu.SMEM),
            pl.BlockSpec(memory_space=pltpu.SMEM),
            pl.BlockSpec((None, hps, nq, HEAD_DIM, tq), lambda bi, hi: (bi, hi, 0, 0, 0)),
            pl.BlockSpec((None, hps, nc, tk, HEAD_DIM), lambda bi, hi: (bi, hi, 0, 0, 0)),
            pl.BlockSpec((None, hps, nc, V_ROWS, tk), lambda bi, hi: (bi, hi, 0, 0, 0)),
            pl.BlockSpec((hps, nt, tk, tq), lambda bi, hi: (hi, 0, 0, 0)),
            pl.BlockSpec((HEAD_DIM, 1), lambda bi, hi: (0, 0)),
        ],
        out_specs=pl.BlockSpec((None, hps, nq, HEAD_DIM, tq), lambda bi, hi: (bi, hi, 0, 0, 0)),
        out_shape=jax.ShapeDtypeStruct((b, h, nq, HEAD_DIM, tq), BF16),
        scratch_shapes=[pltpu.VMEM((nc, tk, tq), F32)],
        compiler_params=_params(2),
        name="attn_diff",
    )(lam, jnp.zeros((1,), jnp.int32), q, k, vt, bias, gsub)


def _attn_b(q, k, vt):
    b, nkv, nq, grp, _, tq = q.shape
    nc, tk = k.shape[2], k.shape[3]
    return pl.pallas_call(
        _attn_b_kernel,
        grid=(b,),
        in_specs=[
            pl.BlockSpec(memory_space=pltpu.SMEM),
            pl.BlockSpec((None, nkv, nq, grp, HEAD_DIM, tq), lambda bi: (bi, 0, 0, 0, 0, 0)),
            pl.BlockSpec((None, nkv, nc, tk, HEAD_DIM), lambda bi: (bi, 0, 0, 0, 0)),
            pl.BlockSpec((None, nkv, nc, V_ROWS, tk), lambda bi: (bi, 0, 0, 0, 0)),
        ],
        out_specs=pl.BlockSpec((None, nkv, nq, grp, HEAD_DIM, tq), lambda bi: (bi, 0, 0, 0, 0, 0)),
        out_shape=jax.ShapeDtypeStruct((b, nkv, nq, grp, HEAD_DIM, tq), BF16),
        scratch_shapes=[pltpu.VMEM((nc, tk, tq), F32)],
        compiler_params=_params(1),
        name="attn_gqa",
    )(jnp.zeros((1,), jnp.int32), q, k, vt)


def _tail_kernel(x_ref, oa_ref, ob_ref, wo_ref, g2_ref, wg_ref, wu_ref, wd_ref, gf_ref, o_ref):
    tn = (((0,), (0,)), ((), ()))
    tm = x_ref.shape[0]
    oa = oa_ref[...].reshape(D_MIX_A, tm)
    ob = ob_ref[...].reshape(D_MIX_B, tm)
    x2 = (x_ref[...]
          + lax.dot_general(oa, wo_ref[0:D_MIX_A, :], tn, preferred_element_type=F32)
          + lax.dot_general(ob, wo_ref[D_MIX_A:, :], tn, preferred_element_type=F32))
    h = _rms_rows(x2, g2_ref[...]).astype(BF16)
    y = x2 + 0.5 * _swiglu_acc(h, wg_ref, wu_ref, wd_ref)
    o_ref[...] = _rms_rows(y, gf_ref[...])


def _tail(x1, oa, ob, wo, g2, wg, wu, wd, gf):
    b, s, _ = x1.shape
    tm = ATTN_Q_TILE
    return pl.pallas_call(
        _tail_kernel,
        grid=(b, s // tm),
        in_specs=[
            pl.BlockSpec((None, tm, D_MODEL), lambda bi, si: (bi, si, 0)),
            pl.BlockSpec((None, N_HEADS_A, None, HEAD_DIM, tm), lambda bi, si: (bi, 0, si, 0, 0)),
            pl.BlockSpec((None, N_KV_B, None, GQA_GROUP, HEAD_DIM, tm), lambda bi, si: (bi, 0, si, 0, 0, 0)),
            _resident((D_MODEL, D_MODEL)),
            _resident((1, D_MODEL)),
            _resident((D_MODEL, D_FF)),
            _resident((D_MODEL, D_FF)),
            _resident((D_FF, D_MODEL)),
            _resident((1, D_MODEL)),
        ],
        out_specs=pl.BlockSpec((None, tm, D_MODEL), lambda bi, si: (bi, si, 0)),
        out_shape=jax.ShapeDtypeStruct((b, s, D_MODEL), F32),
        compiler_params=_params(2),
        name="out_proj_ffn2",
    )(x1, oa, ob, wo, g2, wg, wu, wd, gf)


def _t5_bucket(rel):
    nb = N_BUCKETS // 2
    max_exact = nb // 2
    ret = jnp.where(rel > 0, nb, 0)
    n = jnp.abs(rel)
    nf = jnp.maximum(n, 1).astype(jnp.float32)
    large = max_exact + (jnp.log(nf / max_exact) / math.log(MAX_DISTANCE / max_exact)
                         * (nb - max_exact)).astype(jnp.int32)
    large = jnp.minimum(large, nb - 1)
    return ret + jnp.where(n < max_exact, n, large)


def _bias_base_rows(rel_bias, tq, tk):
    nt = tq // tk + 4
    delta = jnp.arange(-2, nt - 2, dtype=jnp.int32)[:, None] * tk
    y = jnp.arange(tq + tk, dtype=jnp.int32)[None, :]
    table = (rel_bias.astype(F32) * LOG2E).T
    bucket = _t5_bucket(delta + tk - y)
    hit = bucket[None, None] == jnp.arange(N_BUCKETS, dtype=jnp.int32)[None, :, None, None]
    base = jnp.sum(jnp.where(hit, table[:, :, None, None], 0.0), axis=1)
    return base[:, :, None, :]


def _rope_tables_t(s):
    rows = s // GRID_W
    row = jnp.repeat(jnp.arange(rows, dtype=jnp.int32), GRID_W).astype(F32)
    col = jnp.tile(jnp.arange(GRID_W, dtype=jnp.int32), rows).astype(F32)
    inv = ROPE_THETA ** (-jnp.arange(0, ROPE_AXIS_DIM, 2, dtype=F32) / ROPE_AXIS_DIM)
    ang = jnp.concatenate([row[:, None] * inv[None], col[:, None] * inv[None]], axis=-1)
    return jnp.cos(ang).T, jnp.sin(ang).T


def kernel(x, ffn1_norm, ffn1_w_gate, ffn1_w_up, ffn1_w_down, mix_norm, w_in, lambda_q1, lambda_k1, lambda_q2, lambda_k2, diff_subln, q_norm, k_norm, rel_bias, w_out, ffn2_norm, ffn2_w_gate, ffn2_w_up, ffn2_w_down, final_norm):
    b, s, d = x.shape
    tq, tk = ATTN_Q_TILE, ATTN_K_TILE
    lyr = 0
    lambda_init = 0.8 - 0.6 * math.exp(-0.3 * lyr)

    def deinterleave(t, n_heads):
        lead = t.shape[:-1]
        t = t.reshape(lead + (n_heads, HEAD_DIM // 2, 2))
        return jnp.swapaxes(t, -1, -2).reshape(lead + (n_heads * HEAD_DIM,))

    w = w_in[lyr]
    w_in_t = jnp.concatenate(
        [w[:, :1536], deinterleave(w[:, 1536:2048], N_HEADS_B), deinterleave(w[:, 2048:2176], N_KV_B),
         w[:, 2176:]], axis=1).T.astype(BF16)
    gq = deinterleave(q_norm[lyr].astype(F32), 1)[:, None]
    gk = deinterleave(k_norm[lyr].astype(F32), 1)[:, None]
    cos_t, sin_t = _rope_tables_t(s)
    bias = _bias_tiles(_bias_base_rows(rel_bias, tq, tk), tq, tk)
    lam = (jnp.exp(jnp.sum(lambda_q1[lyr].astype(F32) * lambda_k1[lyr].astype(F32)))
           - jnp.exp(jnp.sum(lambda_q2[lyr].astype(F32) * lambda_k2[lyr].astype(F32)))
           + lambda_init).reshape(1, 1)
    gsub = (diff_subln[lyr].astype(F32) * (1.0 - lambda_init))[:, None]

    x1 = _ffn(x.reshape(b * s, d), ffn1_norm[lyr][None, :],
              ffn1_w_gate[lyr].astype(BF16), ffn1_w_up[lyr].astype(BF16), ffn1_w_down[lyr].astype(BF16))
    x1 = x1.reshape(b, s, d)

    qa, ka, va, qb, kb, vb = _proj(x1, mix_norm[lyr][None, :], w_in_t, cos_t, sin_t, gq, gk)

    oa = _attn_a(lam, qa, ka, va, bias, gsub)
    ob = _attn_b(qb, kb, vb)

    return _tail(x1, oa, ob, w_out[lyr].astype(BF16), ffn2_norm[lyr][None, :],
                 ffn2_w_gate[lyr].astype(BF16), ffn2_w_up[lyr].astype(BF16), ffn2_w_down[lyr].astype(BF16),
                 final_norm[None, :])
```

```python
import math

import jax
import jax.numpy as jnp
from jax import lax
from jax.experimental import pallas as pl
from jax.experimental.pallas import tpu as pltpu

D_MODEL = 1024
HEAD_DIM = 64
D_MIX_A = 512
D_MIX_B = 512
N_HEADS_A = 8
DIFF_QK_DIM = 32
N_HEADS_B = 8
GQA_GROUP = 4
N_KV_B = 2
D_FF = 2816
GRID_W = 64
ROPE_THETA = 10000.0
ROPE_AXIS_DIM = 32
N_BUCKETS = 32
MAX_DISTANCE = 128
EPS = 1e-6
D_IN_PROJ = 2304
LOG2E = math.log2(math.e)

VMEM_LIMIT_BYTES = 56 * 1024 * 1024
BF16_SUBLANE_TILE = 16

FFN_TOKEN_TILE = 1024
FFN_CHUNKS = ((0, 1024), (1024, 2048), (2048, 2816))
ATTN_Q_TILE = 512
ATTN_K_TILE = 256
ATTN_A_TILES_PER_ITER = 2
ATTN_A_HEADS_PER_STEP = 4
V_ROWS = HEAD_DIM + BF16_SUBLANE_TILE

BF16 = jnp.bfloat16
F32 = jnp.float32


def _dot(a, b):
    return jnp.dot(a, b, preferred_element_type=F32)


def _rms_rows(x, g):
    ms = jnp.mean(x * x, axis=-1, keepdims=True)
    return x * lax.rsqrt(ms + EPS) * g


def _swiglu_acc(h, wg_ref, wu_ref, wd_ref):
    y = None
    for s, e in FFN_CHUNKS:
        g = _dot(h, wg_ref[:, s:e])
        u = _dot(h, wu_ref[:, s:e])
        a = (g / (1.0 + jnp.exp(-g)) * u).astype(BF16)
        part = _dot(a, wd_ref[s:e, :])
        y = part if y is None else y + part
    return y


def _resident(shape):
    return pl.BlockSpec(shape, lambda *_: (0,) * len(shape), pipeline_mode=pl.Buffered(1))


def _params(n_grid_axes):
    return pltpu.CompilerParams(dimension_semantics=("arbitrary",) * n_grid_axes,
                                vmem_limit_bytes=VMEM_LIMIT_BYTES)


def _toeplitz_kernel(base_ref, o_ref):
    nt, tk, tq = o_ref.shape
    for d in range(nt):
        rows = jnp.broadcast_to(base_ref[d], (tk, tq + tk))
        skew = pltpu.roll(rows, 0, 1, stride=1, stride_axis=0)
        o_ref[d] = skew[:, tk:tk + tq]


def _bias_tiles(base, tq, tk):
    h, nt = base.shape[:2]
    return pl.pallas_call(
        _toeplitz_kernel,
        grid=(h,),
        in_specs=[pl.BlockSpec((None, nt, 1, tq + tk), lambda hi: (hi, 0, 0, 0))],
        out_specs=pl.BlockSpec((None, nt, tk, tq), lambda hi: (hi, 0, 0, 0)),
        out_shape=jax.ShapeDtypeStruct((h, nt, tk, tq), F32),
        compiler_params=_params(1),
        name="bias_tiles",
    )(base)


def _ffn_kernel(x_ref, g_ref, wg_ref, wu_ref, wd_ref, o_ref):
    x = x_ref[...]
    h = _rms_rows(x, g_ref[...]).astype(BF16)
    o_ref[...] = x + 0.5 * _swiglu_acc(h, wg_ref, wu_ref, wd_ref)


def _ffn(x2d, g, wg, wu, wd):
    n = x2d.shape[0]
    tm = FFN_TOKEN_TILE
    return pl.pallas_call(
        _ffn_kernel,
        grid=(n // tm,),
        in_specs=[
            pl.BlockSpec((tm, D_MODEL), lambda i: (i, 0)),
            _resident((1, D_MODEL)),
            _resident((D_MODEL, D_FF)),
            _resident((D_MODEL, D_FF)),
            _resident((D_FF, D_MODEL)),
        ],
        out_specs=pl.BlockSpec((tm, D_MODEL), lambda i: (i, 0)),
        out_shape=jax.ShapeDtypeStruct((n, D_MODEL), F32),
        compiler_params=_params(1),
        name="ffn1",
    )(x2d, g, wg, wu, wd)


def _proj_kernel(x_ref, g_ref, wt_ref, cos_ref, sin_ref, gq_ref, gk_ref,
                 qa_ref, ka_ref, va_ref, qb_ref, kb_ref, vb_ref):
    tk = ATTN_K_TILE
    n_chunks = x_ref.shape[0] // tk
    h = _rms_rows(x_ref[...], g_ref[...]).astype(BF16)
    cos = cos_ref[...]
    sin = sin_ref[...]
    ones = jnp.ones((V_ROWS - HEAD_DIM, tk), BF16)

    def section(r0, r1):
        return lax.dot_general(wt_ref[r0:r1, :], h, (((1,), (1,)), ((), ())),
                               preferred_element_type=F32)

    def norm_rope(t, gain, scale):
        ms = jnp.mean(t * t, axis=0, keepdims=True)
        t = t * lax.rsqrt(ms + EPS) * gain
        te, to = t[:32], t[32:]
        re = te * cos - to * sin
        ro = te * sin + to * cos
        return jnp.concatenate([re, ro], axis=0) * scale

    def put_keys(dst_ref, head, kt):
        for j in range(n_chunks):
            dst_ref[head, j] = kt[:, j * tk:(j + 1) * tk].T.astype(BF16)

    def put_values(dst_ref, head, vt):
        for j in range(n_chunks):
            dst_ref[head, j, 0:HEAD_DIM, :] = vt[:, j * tk:(j + 1) * tk].astype(BF16)
            dst_ref[head, j, HEAD_DIM:V_ROWS, :] = ones

    def head_rows(t, head):
        return t[head * HEAD_DIM:(head + 1) * HEAD_DIM]

    qa = section(0, 512) * (DIFF_QK_DIM ** -0.5 * LOG2E)
    for hh in range(N_HEADS_A):
        qa_ref[hh] = head_rows(qa, hh).astype(BF16)
    ka = section(512, 1024)
    for hh in range(N_HEADS_A):
        put_keys(ka_ref, hh, head_rows(ka, hh))
    va = section(1024, 1536)
    for hh in range(N_HEADS_A):
        put_values(va_ref, hh, head_rows(va, hh))

    qb = section(1536, 2048)
    for hh in range(N_HEADS_B):
        qb_ref[hh // GQA_GROUP, hh % GQA_GROUP] = norm_rope(
            head_rows(qb, hh), gq_ref[...], HEAD_DIM ** -0.5 * LOG2E).astype(BF16)
    kvb = section(2048, 2304)
    for hh in range(N_KV_B):
        put_keys(kb_ref, hh, norm_rope(head_rows(kvb, hh), gk_ref[...], 1.0))
        put_values(vb_ref, hh, head_rows(kvb, N_KV_B + hh))


def _proj(x1, g, w_in_t, cos_t, sin_t, gq, gk):
    b, s, _ = x1.shape
    tm, tk = ATTN_Q_TILE, ATTN_K_TILE
    nq, nc, cpt = s // tm, s // tk, tm // tk
    out_shape = (
        jax.ShapeDtypeStruct((b, N_HEADS_A, nq, HEAD_DIM, tm), BF16),
        jax.ShapeDtypeStruct((b, N_HEADS_A, nc, tk, HEAD_DIM), BF16),
        jax.ShapeDtypeStruct((b, N_HEADS_A, nc, V_ROWS, tk), BF16),
        jax.ShapeDtypeStruct((b, N_KV_B, nq, GQA_GROUP, HEAD_DIM, tm), BF16),
        jax.ShapeDtypeStruct((b, N_KV_B, nc, tk, HEAD_DIM), BF16),
        jax.ShapeDtypeStruct((b, N_KV_B, nc, V_ROWS, tk), BF16),
    )
    out_specs = (
        pl.BlockSpec((None, N_HEADS_A, None, HEAD_DIM, tm), lambda bi, si: (bi, 0, si, 0, 0)),
        pl.BlockSpec((None, N_HEADS_A, cpt, tk, HEAD_DIM), lambda bi, si: (bi, 0, si, 0, 0)),
        pl.BlockSpec((None, N_HEADS_A, cpt, V_ROWS, tk), lambda bi, si: (bi, 0, si, 0, 0)),
        pl.BlockSpec((None, N_KV_B, None, GQA_GROUP, HEAD_DIM, tm), lambda bi, si: (bi, 0, si, 0, 0, 0)),
        pl.BlockSpec((None, N_KV_B, cpt, tk, HEAD_DIM), lambda bi, si: (bi, 0, si, 0, 0)),
        pl.BlockSpec((None, N_KV_B, cpt, V_ROWS, tk), lambda bi, si: (bi, 0, si, 0, 0)),
    )
    return pl.pallas_call(
        _proj_kernel,
        grid=(b, nq),
        in_specs=[
            pl.BlockSpec((None, tm, D_MODEL), lambda bi, si: (bi, si, 0)),
            _resident((1, D_MODEL)),
            _resident((D_IN_PROJ, D_MODEL)),
            pl.BlockSpec((ROPE_AXIS_DIM, tm), lambda bi, si: (0, si)),
            pl.BlockSpec((ROPE_AXIS_DIM, tm), lambda bi, si: (0, si)),
            _resident((HEAD_DIM, 1)),
            _resident((HEAD_DIM, 1)),
        ],
        out_specs=out_specs,
        out_shape=out_shape,
        compiler_params=_params(2),
        name="in_proj",
    )(x1, g, w_in_t, cos_t, sin_t, gq, gk)


def _fold8(x, op):
    rows, cols = x.shape
    return op(x.reshape(rows // 8, 8, cols), axis=0)


class _NoBias:
    def chunk(self, j, first):
        return j

    def tile(self, j, first, head):
        return None

    def shift(self, j, first, head):
        return None


class _RelativeBias:
    def __init__(self, bias_ref, nc, ratio):
        self.bias_ref, self.nc, self.ratio = bias_ref, nc, ratio
        self.nt = bias_ref.shape[1]

    def chunk(self, j, first):
        c = first + (j - 1)
        return jnp.where(c < 0, c + self.nc, jnp.where(c >= self.nc, c - self.nc, c))

    def tile(self, j, first, head):
        if j - 1 > self.ratio:
            return None
        c = first + (j - 1)
        d = jnp.where(c < 0, self.nt - 1, jnp.where(c >= self.nc, 0, j + 1))
        return self.bias_ref[head, d]

    def shift(self, j, first, head):
        if j - 1 <= self.ratio:
            return None
        row_before = self.bias_ref[head, 0, 0:1, :]
        row_after = self.bias_ref[head, self.nt - 1, 0:1, :]
        return jnp.where(first + (j - 1) >= self.nc, row_before, row_after)


def _stream_step(k_ref, vt_ref, s_ref, slot0, plan, prev, new):
    nc = k_ref.shape[1]
    m8 = None
    acc = None
    for j in range(nc):
        if prev is not None:
            prev_max, prev_first, prev_head = prev
            s_prev = s_ref[slot0 + j]
        if new is not None:
            weights, first, head = new
            s = _dot(k_ref[head, plan.chunk(j, first)], weights)
            tile = plan.tile(j, first, head)
            if tile is not None:
                s = s + tile
            s_ref[slot0 + j] = s
            cm = _fold8(s, jnp.max)
            shift = plan.shift(j, first, head)
            if shift is not None:
                cm = cm + shift
            m8 = cm if m8 is None else jnp.maximum(m8, cm)
        if prev is not None:
            shift = plan.shift(j, prev_first, prev_head)
            p = jnp.exp2(s_prev - (prev_max if shift is None else prev_max - shift))
            pv = _dot(vt_ref[prev_head, plan.chunk(j, prev_first)], p.astype(BF16))
            acc = pv if acc is None else acc + pv
    new_max = None if m8 is None else jnp.max(m8, axis=0, keepdims=True)
    return acc, new_max


def _normalized(acc):
    return acc[:HEAD_DIM] / acc[HEAD_DIM:HEAD_DIM + 1]


def _split_index(it, inner):
    assert inner & (inner - 1) == 0
    return lax.shift_right_logical(it, inner.bit_length() - 1), it & (inner - 1)


def _attn_a_kernel(lam_ref, zero_ref, q_ref, k_ref, vt_ref, bias_ref, gsub_ref, o_ref, s_ref):
    heads, nq, _, tq = q_ref.shape
    nc = k_ref.shape[1]
    ratio = tq // ATTN_K_TILE
    slot0 = zero_ref[0]
    plan = _RelativeBias(bias_ref, nc, ratio)
    row = lax.broadcasted_iota(jnp.int32, (HEAD_DIM, tq), 0)
    n_tiles = heads * nq

    def weights(head, t, part):
        qt = q_ref[head, t]
        keep = (row < DIFF_QK_DIM) if part == 0 else (row >= DIFF_QK_DIM)
        return jnp.where(keep, qt, jnp.zeros_like(qt))

    def step(prev, new):
        return _stream_step(k_ref, vt_ref, s_ref, slot0, plan, prev, new)

    _, m0 = step(None, (weights(0, 0, 0), 0, 0))

    def body(i, m):
        for u in range(ATTN_A_TILES_PER_ITER):
            it = i * ATTN_A_TILES_PER_ITER + u
            head, t = _split_index(it, nq)
            head_n, t_n = _split_index(jnp.minimum(it + 1, n_tiles - 1), nq)
            acc1, m = step((m, t * ratio, head), (weights(head, t, 1), t * ratio, head))
            acc2, m = step((m, t * ratio, head), (weights(head_n, t_n, 0), t_n * ratio, head_n))
            oa = _normalized(acc1) - lam_ref[0, 0] * _normalized(acc2)
            ms = jnp.mean(oa * oa, axis=0, keepdims=True)
            o_ref[head, t] = (oa * lax.rsqrt(ms + EPS) * gsub_ref[...]).astype(BF16)
        return m

    lax.fori_loop(0, n_tiles // ATTN_A_TILES_PER_ITER, body, m0)


def _attn_b_kernel(zero_ref, q_ref, k_ref, vt_ref, o_ref, s_ref):
    heads, nq = q_ref.shape[:2]
    slot0 = zero_ref[0]
    plan = _NoBias()
    n_tiles = heads * nq

    def step(prev, new):
        return _stream_step(k_ref, vt_ref, s_ref, slot0, plan, prev, new)

    _, m0 = step(None, (q_ref[0, 0, 0], 0, 0))

    def body(it, m):
        head, t = _split_index(it, nq)
        head_n, t_n = _split_index(jnp.minimum(it + 1, n_tiles - 1), nq)
        for g in range(GQA_GROUP):
            if g + 1 < GQA_GROUP:
                new = (q_ref[head, t, g + 1], 0, head)
            else:
                new = (q_ref[head_n, t_n, 0], 0, head_n)
            acc, m = step((m, 0, head), new)
            o_ref[head, t, g] = _normalized(acc).astype(BF16)
        return m

    lax.fori_loop(0, n_tiles, body, m0)


def _attn_a(lam, q, k, vt, bias, gsub):
    b, h, nq, _, tq = q.shape
    nc, tk = k.shape[2], k.shape[3]
    nt = bias.shape[1]
    hps = ATTN_A_HEADS_PER_STEP
    return pl.pallas_call(
        _attn_a_kernel,
        grid=(b, h // hps),
        in_specs=[
            pl.BlockSpec(memory_space=pltpu.SMEM),
            pl.BlockSpec(memory_space=pltpu.SMEM),
            pl.BlockSpec((None, hps, nq, HEAD_DIM, tq), lambda bi, hi: (bi, hi, 0, 0, 0)),
            pl.BlockSpec((None, hps, nc, tk, HEAD_DIM), lambda bi, hi: (bi, hi, 0, 0, 0)),
            pl.BlockSpec((None, hps, nc, V_ROWS, tk), lambda bi, hi: (bi, hi, 0, 0, 0)),
            pl.BlockSpec((hps, nt, tk, tq), lambda bi, hi: (hi, 0, 0, 0)),
            pl.BlockSpec((HEAD_DIM, 1), lambda bi, hi: (0, 0)),
        ],
        out_specs=pl.BlockSpec((None, hps, nq, HEAD_DIM, tq), lambda bi, hi: (bi, hi, 0, 0, 0)),
        out_shape=jax.ShapeDtypeStruct((b, h, nq, HEAD_DIM, tq), BF16),
        scratch_shapes=[pltpu.VMEM((nc, tk, tq), F32)],
        compiler_params=_params(2),
        name="attn_diff",
    )(lam, jnp.zeros((1,), jnp.int32), q, k, vt, bias, gsub)


def _attn_b(q, k, vt):
    b, nkv, nq, grp, _, tq = q.shape
    nc, tk = k.shape[2], k.shape[3]
    return pl.pallas_call(
        _attn_b_kernel,
        grid=(b,),
        in_specs=[
            pl.BlockSpec(memory_space=pltpu.SMEM),
            pl.BlockSpec((None, nkv, nq, grp, HEAD_DIM, tq), lambda bi: (bi, 0, 0, 0, 0, 0)),
            pl.BlockSpec((None, nkv, nc, tk, HEAD_DIM), lambda bi: (bi, 0, 0, 0, 0)),
            pl.BlockSpec((None, nkv, nc, V_ROWS, tk), lambda bi: (bi, 0, 0, 0, 0)),
        ],
        out_specs=pl.BlockSpec((None, nkv, nq, grp, HEAD_DIM, tq), lambda bi: (bi, 0, 0, 0, 0, 0)),
        out_shape=jax.ShapeDtypeStruct((b, nkv, nq, grp, HEAD_DIM, tq), BF16),
        scratch_shapes=[pltpu.VMEM((nc, tk, tq), F32)],
        compiler_params=_params(1),
        name="attn_gqa",
    )(jnp.zeros((1,), jnp.int32), q, k, vt)


def _tail_kernel(x_ref, oa_ref, ob_ref, wo_ref, g2_ref, wg_ref, wu_ref, wd_ref, gf_ref, o_ref):
    tn = (((0,), (0,)), ((), ()))
    n_tiles, tq = oa_ref.shape[1], oa_ref.shape[-1]
    mixed = []
    for i in range(n_tiles):
        oa = oa_ref[:, i].reshape(D_MIX_A, tq)
        ob = ob_ref[:, i].reshape(D_MIX_B, tq)
        mixed.append(lax.dot_general(oa, wo_ref[0:D_MIX_A, :], tn, preferred_element_type=F32)
                     + lax.dot_general(ob, wo_ref[D_MIX_A:, :], tn, preferred_element_type=F32))
    x2 = x_ref[...] + (mixed[0] if n_tiles == 1 else jnp.concatenate(mixed, axis=0))
    h = _rms_rows(x2, g2_ref[...]).astype(BF16)
    y = x2 + 0.5 * _swiglu_acc(h, wg_ref, wu_ref, wd_ref)
    o_ref[...] = _rms_rows(y, gf_ref[...])


def _tail(x1, oa, ob, wo, g2, wg, wu, wd, gf):
    b, s, _ = x1.shape
    tm, tq = FFN_TOKEN_TILE, ATTN_Q_TILE
    tpb = tm // tq
    return pl.pallas_call(
        _tail_kernel,
        grid=(b, s // tm),
        in_specs=[
            pl.BlockSpec((None, tm, D_MODEL), lambda bi, si: (bi, si, 0)),
            pl.BlockSpec((None, N_HEADS_A, tpb, HEAD_DIM, tq), lambda bi, si: (bi, 0, si, 0, 0)),
            pl.BlockSpec((None, N_KV_B, tpb, GQA_GROUP, HEAD_DIM, tq), lambda bi, si: (bi, 0, si, 0, 0, 0)),
            _resident((D_MODEL, D_MODEL)),
            _resident((1, D_MODEL)),
            _resident((D_MODEL, D_FF)),
            _resident((D_MODEL, D_FF)),
            _resident((D_FF, D_MODEL)),
            _resident((1, D_MODEL)),
        ],
        out_specs=pl.BlockSpec((None, tm, D_MODEL), lambda bi, si: (bi, si, 0)),
        out_shape=jax.ShapeDtypeStruct((b, s, D_MODEL), F32),
        compiler_params=_params(2),
        name="out_proj_ffn2",
    )(x1, oa, ob, wo, g2, wg, wu, wd, gf)


def _t5_bucket(rel):
    nb = N_BUCKETS // 2
    max_exact = nb // 2
    ret = jnp.where(rel > 0, nb, 0)
    n = jnp.abs(rel)
    nf = jnp.maximum(n, 1).astype(jnp.float32)
    large = max_exact + (jnp.log(nf / max_exact) / math.log(MAX_DISTANCE / max_exact)
                         * (nb - max_exact)).astype(jnp.int32)
    large = jnp.minimum(large, nb - 1)
    return ret + jnp.where(n < max_exact, n, large)


def _bias_base_rows(rel_bias, tq, tk):
    nt = tq // tk + 4
    delta = jnp.arange(-2, nt - 2, dtype=jnp.int32)[:, None] * tk
    y = jnp.arange(tq + tk, dtype=jnp.int32)[None, :]
    table = (rel_bias.astype(F32) * LOG2E).T
    bucket = _t5_bucket(delta + tk - y)
    hit = bucket[None, None] == jnp.arange(N_BUCKETS, dtype=jnp.int32)[None, :, None, None]
    base = jnp.sum(jnp.where(hit, table[:, :, None, None], 0.0), axis=1)
    return base[:, :, None, :]


def _rope_tables_t(s):
    rows = s // GRID_W
    row = jnp.repeat(jnp.arange(rows, dtype=jnp.int32), GRID_W).astype(F32)
    col = jnp.tile(jnp.arange(GRID_W, dtype=jnp.int32), rows).astype(F32)
    inv = ROPE_THETA ** (-jnp.arange(0, ROPE_AXIS_DIM, 2, dtype=F32) / ROPE_AXIS_DIM)
    ang = jnp.concatenate([row[:, None] * inv[None], col[:, None] * inv[None]], axis=-1)
    return jnp.cos(ang).T, jnp.sin(ang).T


def kernel(x, ffn1_norm, ffn1_w_gate, ffn1_w_up, ffn1_w_down, mix_norm, w_in, lambda_q1, lambda_k1, lambda_q2, lambda_k2, diff_subln, q_norm, k_norm, rel_bias, w_out, ffn2_norm, ffn2_w_gate, ffn2_w_up, ffn2_w_down, final_norm):
    b, s, d = x.shape
    tq, tk = ATTN_Q_TILE, ATTN_K_TILE
    lyr = 0
    lambda_init = 0.8 - 0.6 * math.exp(-0.3 * lyr)

    def deinterleave(t, n_heads):
        lead = t.shape[:-1]
        t = t.reshape(lead + (n_heads, HEAD_DIM // 2, 2))
        return jnp.swapaxes(t, -1, -2).reshape(lead + (n_heads * HEAD_DIM,))

    w = w_in[lyr]
    w_in_t = jnp.concatenate(
        [w[:, :1536], deinterleave(w[:, 1536:2048], N_HEADS_B), deinterleave(w[:, 2048:2176], N_KV_B),
         w[:, 2176:]], axis=1).T.astype(BF16)
    gq = deinterleave(q_norm[lyr].astype(F32), 1)[:, None]
    gk = deinterleave(k_norm[lyr].astype(F32), 1)[:, None]
    cos_t, sin_t = _rope_tables_t(s)
    bias = _bias_tiles(_bias_base_rows(rel_bias, tq, tk), tq, tk)
    lam = (jnp.exp(jnp.sum(lambda_q1[lyr].astype(F32) * lambda_k1[lyr].astype(F32)))
           - jnp.exp(jnp.sum(lambda_q2[lyr].astype(F32) * lambda_k2[lyr].astype(F32)))
           + lambda_init).reshape(1, 1)
    gsub = (diff_subln[lyr].astype(F32) * (1.0 - lambda_init))[:, None]

    x1 = _ffn(x.reshape(b * s, d), ffn1_norm[lyr][None, :],
              ffn1_w_gate[lyr].astype(BF16), ffn1_w_up[lyr].astype(BF16), ffn1_w_down[lyr].astype(BF16))
    x1 = x1.reshape(b, s, d)

    qa, ka, va, qb, kb, vb = _proj(x1, mix_norm[lyr][None, :], w_in_t, cos_t, sin_t, gq, gk)

    oa = _attn_a(lam, qa, ka, va, bias, gsub)
    ob = _attn_b(qb, kb, vb)

    return _tail(x1, oa, ob, w_out[lyr].astype(BF16), ffn2_norm[lyr][None, :],
                 ffn2_w_gate[lyr].astype(BF16), ffn2_w_up[lyr].astype(BF16), ffn2_w_down[lyr].astype(BF16),
                 final_norm[None, :])
```

```python
import math

import jax
import jax.numpy as jnp
from jax import lax
from jax.experimental import pallas as pl
from jax.experimental.pallas import tpu as pltpu

D_MODEL = 1024
HEAD_DIM = 64
D_MIX_A = 512
D_MIX_B = 512
N_HEADS_A = 8
DIFF_QK_DIM = 32
N_HEADS_B = 8
GQA_GROUP = 4
N_KV_B = 2
D_FF = 2816
GRID_W = 64
ROPE_THETA = 10000.0
ROPE_AXIS_DIM = 32
N_BUCKETS = 32
MAX_DISTANCE = 128
EPS = 1e-6
D_IN_PROJ = 2304
LOG2E = math.log2(math.e)

VMEM_LIMIT_BYTES = 56 * 1024 * 1024
BF16_SUBLANE_TILE = 16

FFN_TOKEN_TILE = 1024
FFN_CHUNKS = ((0, 1024), (1024, 2048), (2048, 2816))
ATTN_Q_TILE = 512
ATTN_K_TILE = 256
ATTN_A_TILES_PER_ITER = 2
ATTN_A_HEADS_PER_STEP = 4
V_ROWS = HEAD_DIM + BF16_SUBLANE_TILE

BF16 = jnp.bfloat16
F32 = jnp.float32


def _dot(a, b):
    return jnp.dot(a, b, preferred_element_type=F32)


def _rms_rows(x, g):
    ms = jnp.mean(x * x, axis=-1, keepdims=True)
    return x * lax.rsqrt(ms + EPS) * g


def _swiglu_acc(h, wg_ref, wu_ref, wd_ref):
    y = None
    for s, e in FFN_CHUNKS:
        g = _dot(h, wg_ref[:, s:e])
        u = _dot(h, wu_ref[:, s:e])
        a = (g / (1.0 + jnp.exp(-g)) * u).astype(BF16)
        part = _dot(a, wd_ref[s:e, :])
        y = part if y is None else y + part
    return y


def _resident(shape):
    return pl.BlockSpec(shape, lambda *_: (0,) * len(shape), pipeline_mode=pl.Buffered(1))


def _params(n_grid_axes):
    return pltpu.CompilerParams(dimension_semantics=("arbitrary",) * n_grid_axes,
                                vmem_limit_bytes=VMEM_LIMIT_BYTES)


def _toeplitz_kernel(base_ref, o_ref):
    nt, tk, tq = o_ref.shape
    for d in range(nt):
        rows = jnp.broadcast_to(base_ref[d], (tk, tq + tk))
        skew = pltpu.roll(rows, 0, 1, stride=1, stride_axis=0)
        o_ref[d] = skew[:, tk:tk + tq]


def _bias_tiles(base, tq, tk):
    h, nt = base.shape[:2]
    return pl.pallas_call(
        _toeplitz_kernel,
        grid=(h,),
        in_specs=[pl.BlockSpec((None, nt, 1, tq + tk), lambda hi: (hi, 0, 0, 0))],
        out_specs=pl.BlockSpec((None, nt, tk, tq), lambda hi: (hi, 0, 0, 0)),
        out_shape=jax.ShapeDtypeStruct((h, nt, tk, tq), F32),
        compiler_params=_params(1),
        name="bias_tiles",
    )(base)


def _ffn_kernel(x_ref, g_ref, wg_ref, wu_ref, wd_ref, o_ref):
    x = x_ref[...]
    h = _rms_rows(x, g_ref[...]).astype(BF16)
    o_ref[...] = x + 0.5 * _swiglu_acc(h, wg_ref, wu_ref, wd_ref)


def _ffn(x2d, g, wg, wu, wd):
    n = x2d.shape[0]
    tm = FFN_TOKEN_TILE
    return pl.pallas_call(
        _ffn_kernel,
        grid=(n // tm,),
        in_specs=[
            pl.BlockSpec((tm, D_MODEL), lambda i: (i, 0)),
            _resident((1, D_MODEL)),
            _resident((D_MODEL, D_FF)),
            _resident((D_MODEL, D_FF)),
            _resident((D_FF, D_MODEL)),
        ],
        out_specs=pl.BlockSpec((tm, D_MODEL), lambda i: (i, 0)),
        out_shape=jax.ShapeDtypeStruct((n, D_MODEL), F32),
        compiler_params=_params(1),
        name="ffn1",
    )(x2d, g, wg, wu, wd)


def _proj_kernel(x_ref, g_ref, wt_ref, cos_ref, sin_ref, gq_ref, gk_ref,
                 qa_ref, ka_ref, va_ref, qb_ref, kb_ref, vb_ref):
    tk = ATTN_K_TILE
    n_chunks = x_ref.shape[0] // tk
    h = _rms_rows(x_ref[...], g_ref[...]).astype(BF16)
    cos = cos_ref[...]
    sin = sin_ref[...]
    ones = jnp.ones((V_ROWS - HEAD_DIM, tk), BF16)

    def section(r0, r1):
        return lax.dot_general(wt_ref[r0:r1, :], h, (((1,), (1,)), ((), ())),
                               preferred_element_type=F32)

    def norm_rope(t, gain, scale):
        ms = jnp.mean(t * t, axis=0, keepdims=True)
        t = t * lax.rsqrt(ms + EPS) * gain
        te, to = t[:32], t[32:]
        re = te * cos - to * sin
        ro = te * sin + to * cos
        return jnp.concatenate([re, ro], axis=0) * scale

    def put_keys(dst_ref, head, kt):
        for j in range(n_chunks):
            dst_ref[head, j] = kt[:, j * tk:(j + 1) * tk].T.astype(BF16)

    def put_values(dst_ref, head, vt):
        for j in range(n_chunks):
            dst_ref[head, j, 0:HEAD_DIM, :] = vt[:, j * tk:(j + 1) * tk].astype(BF16)
            dst_ref[head, j, HEAD_DIM:V_ROWS, :] = ones

    def head_rows(t, head):
        return t[head * HEAD_DIM:(head + 1) * HEAD_DIM]

    qa = section(0, 512) * (DIFF_QK_DIM ** -0.5 * LOG2E)
    for hh in range(N_HEADS_A):
        qa_ref[hh] = head_rows(qa, hh).astype(BF16)
    ka = section(512, 1024)
    for hh in range(N_HEADS_A):
        put_keys(ka_ref, hh, head_rows(ka, hh))
    va = section(1024, 1536)
    for hh in range(N_HEADS_A):
        put_values(va_ref, hh, head_rows(va, hh))

    qb = section(1536, 2048)
    for hh in range(N_HEADS_B):
        qb_ref[hh // GQA_GROUP, hh % GQA_GROUP] = norm_rope(
            head_rows(qb, hh), gq_ref[...], HEAD_DIM ** -0.5 * LOG2E).astype(BF16)
    kvb = section(2048, 2304)
    for hh in range(N_KV_B):
        put_keys(kb_ref, hh, norm_rope(head_rows(kvb, hh), gk_ref[...], 1.0))
        put_values(vb_ref, hh, head_rows(kvb, N_KV_B + hh))


def _proj(x1, g, w_in_t, cos_t, sin_t, gq, gk):
    b, s, _ = x1.shape
    tm, tk = ATTN_Q_TILE, ATTN_K_TILE
    nq, nc, cpt = s // tm, s // tk, tm // tk
    out_shape = (
        jax.ShapeDtypeStruct((b, N_HEADS_A, nq, HEAD_DIM, tm), BF16),
        jax.ShapeDtypeStruct((b, N_HEADS_A, nc, tk, HEAD_DIM), BF16),
        jax.ShapeDtypeStruct((b, N_HEADS_A, nc, V_ROWS, tk), BF16),
        jax.ShapeDtypeStruct((b, N_KV_B, nq, GQA_GROUP, HEAD_DIM, tm), BF16),
        jax.ShapeDtypeStruct((b, N_KV_B, nc, tk, HEAD_DIM), BF16),
        jax.ShapeDtypeStruct((b, N_KV_B, nc, V_ROWS, tk), BF16),
    )
    out_specs = (
        pl.BlockSpec((None, N_HEADS_A, None, HEAD_DIM, tm), lambda bi, si: (bi, 0, si, 0, 0)),
        pl.BlockSpec((None, N_HEADS_A, cpt, tk, HEAD_DIM), lambda bi, si: (bi, 0, si, 0, 0)),
        pl.BlockSpec((None, N_HEADS_A, cpt, V_ROWS, tk), lambda bi, si: (bi, 0, si, 0, 0)),
        pl.BlockSpec((None, N_KV_B, None, GQA_GROUP, HEAD_DIM, tm), lambda bi, si: (bi, 0, si, 0, 0, 0)),
        pl.BlockSpec((None, N_KV_B, cpt, tk, HEAD_DIM), lambda bi, si: (bi, 0, si, 0, 0)),
        pl.BlockSpec((None, N_KV_B, cpt, V_ROWS, tk), lambda bi, si: (bi, 0, si, 0, 0)),
    )
    return pl.pallas_call(
        _proj_kernel,
        grid=(b, nq),
        in_specs=[
            pl.BlockSpec((None, tm, D_MODEL), lambda bi, si: (bi, si, 0)),
            _resident((1, D_MODEL)),
            _resident((D_IN_PROJ, D_MODEL)),
            pl.BlockSpec((ROPE_AXIS_DIM, tm), lambda bi, si: (0, si)),
            pl.BlockSpec((ROPE_AXIS_DIM, tm), lambda bi, si: (0, si)),
            _resident((HEAD_DIM, 1)),
            _resident((HEAD_DIM, 1)),
        ],
        out_specs=out_specs,
        out_shape=out_shape,
        compiler_params=_params(2),
        name="in_proj",
    )(x1, g, w_in_t, cos_t, sin_t, gq, gk)


def _fold8(x, op):
    rows, cols = x.shape
    return op(x.reshape(rows // 8, 8, cols), axis=0)


class _NoBias:
    qk_ahead = 1

    def chunk(self, j, first):
        return j

    def tile(self, j, first, head):
        return None

    def shift(self, j, first, head):
        return None


class _RelativeBias:
    qk_ahead = 0

    def __init__(self, bias_ref, nc, ratio):
        self.bias_ref, self.nc, self.ratio = bias_ref, nc, ratio
        self.nt = bias_ref.shape[1]

    def chunk(self, j, first):
        c = first + (j - 1)
        return jnp.where(c < 0, c + self.nc, jnp.where(c >= self.nc, c - self.nc, c))

    def tile(self, j, first, head):
        if j - 1 > self.ratio:
            return None
        c = first + (j - 1)
        d = jnp.where(c < 0, self.nt - 1, jnp.where(c >= self.nc, 0, j + 1))
        return self.bias_ref[head, d]

    def shift(self, j, first, head):
        if j - 1 <= self.ratio:
            return None
        row_before = self.bias_ref[head, 0, 0:1, :]
        row_after = self.bias_ref[head, self.nt - 1, 0:1, :]
        return jnp.where(first + (j - 1) >= self.nc, row_before, row_after)


def _stream_step(k_ref, vt_ref, s_ref, slot0, plan, prev, new):
    nc = k_ref.shape[1]
    m8 = None
    acc = None
    ahead = []
    if new is not None:
        weights, first, head = new
        for j in range(min(plan.qk_ahead, nc)):
            ahead.append(_dot(k_ref[head, plan.chunk(j, first)], weights))
    for j in range(nc):
        if prev is not None:
            prev_max, prev_first, prev_head = prev
            s_prev = s_ref[slot0 + j]
        if new is not None:
            if j + plan.qk_ahead < nc:
                ahead.append(_dot(k_ref[head, plan.chunk(j + plan.qk_ahead, first)], weights))
            s = ahead.pop(0)
            tile = plan.tile(j, first, head)
            if tile is not None:
                s = s + tile
            s_ref[slot0 + j] = s
            cm = _fold8(s, jnp.max)
            shift = plan.shift(j, first, head)
            if shift is not None:
                cm = cm + shift
            m8 = cm if m8 is None else jnp.maximum(m8, cm)
        if prev is not None:
            shift = plan.shift(j, prev_first, prev_head)
            p = jnp.exp2(s_prev - (prev_max if shift is None else prev_max - shift))
            pv = _dot(vt_ref[prev_head, plan.chunk(j, prev_first)], p.astype(BF16))
            acc = pv if acc is None else acc + pv
    new_max = None if m8 is None else jnp.max(m8, axis=0, keepdims=True)
    return acc, new_max


def _normalized(acc):
    return acc[:HEAD_DIM] / acc[HEAD_DIM:HEAD_DIM + 1]


def _split_index(it, inner):
    assert inner & (inner - 1) == 0
    return lax.shift_right_logical(it, inner.bit_length() - 1), it & (inner - 1)


def _attn_a_kernel(lam_ref, zero_ref, q_ref, k_ref, vt_ref, bias_ref, gsub_ref, o_ref, s_ref):
    heads, nq, _, tq = q_ref.shape
    nc = k_ref.shape[1]
    ratio = tq // ATTN_K_TILE
    slot0 = zero_ref[0]
    plan = _RelativeBias(bias_ref, nc, ratio)
    row = lax.broadcasted_iota(jnp.int32, (HEAD_DIM, tq), 0)
    n_tiles = heads * nq

    def weights(head, t, part):
        qt = q_ref[head, t]
        keep = (row < DIFF_QK_DIM) if part == 0 else (row >= DIFF_QK_DIM)
        return jnp.where(keep, qt, jnp.zeros_like(qt))

    def step(prev, new):
        return _stream_step(k_ref, vt_ref, s_ref, slot0, plan, prev, new)

    _, m0 = step(None, (weights(0, 0, 0), 0, 0))

    def body(i, m):
        for u in range(ATTN_A_TILES_PER_ITER):
            it = i * ATTN_A_TILES_PER_ITER + u
            head, t = _split_index(it, nq)
            head_n, t_n = _split_index(jnp.minimum(it + 1, n_tiles - 1), nq)
            acc1, m = step((m, t * ratio, head), (weights(head, t, 1), t * ratio, head))
            acc2, m = step((m, t * ratio, head), (weights(head_n, t_n, 0), t_n * ratio, head_n))
            oa = _normalized(acc1) - lam_ref[0, 0] * _normalized(acc2)
            ms = jnp.mean(oa * oa, axis=0, keepdims=True)
            o_ref[head, t] = (oa * lax.rsqrt(ms + EPS) * gsub_ref[...]).astype(BF16)
        return m

    lax.fori_loop(0, n_tiles // ATTN_A_TILES_PER_ITER, body, m0)


def _attn_b_kernel(zero_ref, q_ref, k_ref, vt_ref, o_ref, s_ref):
    heads, nq = q_ref.shape[:2]
    slot0 = zero_ref[0]
    plan = _NoBias()
    n_tiles = heads * nq

    def step(prev, new):
        return _stream_step(k_ref, vt_ref, s_ref, slot0, plan, prev, new)

    _, m0 = step(None, (q_ref[0, 0, 0], 0, 0))

    def body(it, m):
        head, t = _split_index(it, nq)
        head_n, t_n = _split_index(jnp.minimum(it + 1, n_tiles - 1), nq)
        for g in range(GQA_GROUP):
            if g + 1 < GQA_GROUP:
                new = (q_ref[head, t, g + 1], 0, head)
            else:
                new = (q_ref[head_n, t_n, 0], 0, head_n)
            acc, m = step((m, 0, head), new)
            o_ref[head, t, g] = _normalized(acc).astype(BF16)
        return m

    lax.fori_loop(0, n_tiles, body, m0)


def _attn_a(lam, q, k, vt, bias, gsub):
    b, h, nq, _, tq = q.shape
    nc, tk = k.shape[2], k.shape[3]
    nt = bias.shape[1]
    hps = ATTN_A_HEADS_PER_STEP
    return pl.pallas_call(
        _attn_a_kernel,
        grid=(b, h // hps),
        in_specs=[
            pl.BlockSpec(memory_space=pltpu.SMEM),
            pl.BlockSpec(memory_space=pltpu.SMEM),
            pl.BlockSpec((None, hps, nq, HEAD_DIM, tq), lambda bi, hi: (bi, hi, 0, 0, 0)),
            pl.BlockSpec((None, hps, nc, tk, HEAD_DIM), lambda bi, hi: (bi, hi, 0, 0, 0)),
            pl.BlockSpec((None, hps, nc, V_ROWS, tk), lambda bi, hi: (bi, hi, 0, 0, 0)),
            pl.BlockSpec((hps, nt, tk, tq), lambda bi, hi: (hi, 0, 0, 0)),
            pl.BlockSpec((HEAD_DIM, 1), lambda bi, hi: (0, 0)),
        ],
        out_specs=pl.BlockSpec((None, hps, nq, HEAD_DIM, tq), lambda bi, hi: (bi, hi, 0, 0, 0)),
        out_shape=jax.ShapeDtypeStruct((b, h, nq, HEAD_DIM, tq), BF16),
        scratch_shapes=[pltpu.VMEM((nc, tk, tq), F32)],
        compiler_params=_params(2),
        name="attn_diff",
    )(lam, jnp.zeros((1,), jnp.int32), q, k, vt, bias, gsub)


def _attn_b(q, k, vt):
    b, nkv, nq, grp, _, tq = q.shape
    nc, tk = k.shape[2], k.shape[3]
    return pl.pallas_call(
        _attn_b_kernel,
        grid=(b,),
        in_specs=[
            pl.BlockSpec(memory_space=pltpu.SMEM),
            pl.BlockSpec((None, nkv, nq, grp, HEAD_DIM, tq), lambda bi: (bi, 0, 0, 0, 0, 0)),
            pl.BlockSpec((None, nkv, nc, tk, HEAD_DIM), lambda bi: (bi, 0, 0, 0, 0)),
            pl.BlockSpec((None, nkv, nc, V_ROWS, tk), lambda bi: (bi, 0, 0, 0, 0)),
        ],
        out_specs=pl.BlockSpec((None, nkv, nq, grp, HEAD_DIM, tq), lambda bi: (bi, 0, 0, 0, 0, 0)),
        out_shape=jax.ShapeDtypeStruct((b, nkv, nq, grp, HEAD_DIM, tq), BF16),
        scratch_shapes=[pltpu.VMEM((nc, tk, tq), F32)],
        compiler_params=_params(1),
        name="attn_gqa",
    )(jnp.zeros((1,), jnp.int32), q, k, vt)


def _tail_kernel(x_ref, oa_ref, ob_ref, wo_ref, g2_ref, wg_ref, wu_ref, wd_ref, gf_ref, o_ref):
    tn = (((0,), (0,)), ((), ()))
    n_tiles, tq = oa_ref.shape[1], oa_ref.shape[-1]
    mixed = []
    for i in range(n_tiles):
        oa = oa_ref[:, i].reshape(D_MIX_A, tq)
        ob = ob_ref[:, i].reshape(D_MIX_B, tq)
        mixed.append(lax.dot_general(oa, wo_ref[0:D_MIX_A, :], tn, preferred_element_type=F32)
                     + lax.dot_general(ob, wo_ref[D_MIX_A:, :], tn, preferred_element_type=F32))
    x2 = x_ref[...] + (mixed[0] if n_tiles == 1 else jnp.concatenate(mixed, axis=0))
    h = _rms_rows(x2, g2_ref[...]).astype(BF16)
    y = x2 + 0.5 * _swiglu_acc(h, wg_ref, wu_ref, wd_ref)
    o_ref[...] = _rms_rows(y, gf_ref[...])


def _tail(x1, oa, ob, wo, g2, wg, wu, wd, gf):
    b, s, _ = x1.shape
    tm, tq = FFN_TOKEN_TILE, ATTN_Q_TILE
    tpb = tm // tq
    return pl.pallas_call(
        _tail_kernel,
        grid=(b, s // tm),
        in_specs=[
            pl.BlockSpec((None, tm, D_MODEL), lambda bi, si: (bi, si, 0)),
            pl.BlockSpec((None, N_HEADS_A, tpb, HEAD_DIM, tq), lambda bi, si: (bi, 0, si, 0, 0)),
            pl.BlockSpec((None, N_KV_B, tpb, GQA_GROUP, HEAD_DIM, tq), lambda bi, si: (bi, 0, si, 0, 0, 0)),
            _resident((D_MODEL, D_MODEL)),
            _resident((1, D_MODEL)),
            _resident((D_MODEL, D_FF)),
            _resident((D_MODEL, D_FF)),
            _resident((D_FF, D_MODEL)),
            _resident((1, D_MODEL)),
        ],
        out_specs=pl.BlockSpec((None, tm, D_MODEL), lambda bi, si: (bi, si, 0)),
        out_shape=jax.ShapeDtypeStruct((b, s, D_MODEL), F32),
        compiler_params=_params(2),
        name="out_proj_ffn2",
    )(x1, oa, ob, wo, g2, wg, wu, wd, gf)


def _t5_bucket(rel):
    nb = N_BUCKETS // 2
    max_exact = nb // 2
    ret = jnp.where(rel > 0, nb, 0)
    n = jnp.abs(rel)
    nf = jnp.maximum(n, 1).astype(jnp.float32)
    large = max_exact + (jnp.log(nf / max_exact) / math.log(MAX_DISTANCE / max_exact)
                         * (nb - max_exact)).astype(jnp.int32)
    large = jnp.minimum(large, nb - 1)
    return ret + jnp.where(n < max_exact, n, large)


def _bias_base_rows(rel_bias, tq, tk):
    nt = tq // tk + 4
    delta = jnp.arange(-2, nt - 2, dtype=jnp.int32)[:, None] * tk
    y = jnp.arange(tq + tk, dtype=jnp.int32)[None, :]
    table = (rel_bias.astype(F32) * LOG2E).T
    bucket = _t5_bucket(delta + tk - y)
    hit = bucket[None, None] == jnp.arange(N_BUCKETS, dtype=jnp.int32)[None, :, None, None]
    base = jnp.sum(jnp.where(hit, table[:, :, None, None], 0.0), axis=1)
    return base[:, :, None, :]


def _rope_tables_t(s):
    rows = s // GRID_W
    row = jnp.repeat(jnp.arange(rows, dtype=jnp.int32), GRID_W).astype(F32)
    col = jnp.tile(jnp.arange(GRID_W, dtype=jnp.int32), rows).astype(F32)
    inv = ROPE_THETA ** (-jnp.arange(0, ROPE_AXIS_DIM, 2, dtype=F32) / ROPE_AXIS_DIM)
    ang = jnp.concatenate([row[:, None] * inv[None], col[:, None] * inv[None]], axis=-1)
    return jnp.cos(ang).T, jnp.sin(ang).T


def kernel(x, ffn1_norm, ffn1_w_gate, ffn1_w_up, ffn1_w_down, mix_norm, w_in, lambda_q1, lambda_k1, lambda_q2, lambda_k2, diff_subln, q_norm, k_norm, rel_bias, w_out, ffn2_norm, ffn2_w_gate, ffn2_w_up, ffn2_w_down, final_norm):
    b, s, d = x.shape
    tq, tk = ATTN_Q_TILE, ATTN_K_TILE
    lyr = 0
    lambda_init = 0.8 - 0.6 * math.exp(-0.3 * lyr)

    def deinterleave(t, n_heads):
        lead = t.shape[:-1]
        t = t.reshape(lead + (n_heads, HEAD_DIM // 2, 2))
        return jnp.swapaxes(t, -1, -2).reshape(lead + (n_heads * HEAD_DIM,))

    w = w_in[lyr]
    w_in_t = jnp.concatenate(
        [w[:, :1536], deinterleave(w[:, 1536:2048], N_HEADS_B), deinterleave(w[:, 2048:2176], N_KV_B),
         w[:, 2176:]], axis=1).T.astype(BF16)
    gq = deinterleave(q_norm[lyr].astype(F32), 1)[:, None]
    gk = deinterleave(k_norm[lyr].astype(F32), 1)[:, None]
    cos_t, sin_t = _rope_tables_t(s)
    bias = _bias_tiles(_bias_base_rows(rel_bias, tq, tk), tq, tk)
    lam = (jnp.exp(jnp.sum(lambda_q1[lyr].astype(F32) * lambda_k1[lyr].astype(F32)))
           - jnp.exp(jnp.sum(lambda_q2[lyr].astype(F32) * lambda_k2[lyr].astype(F32)))
           + lambda_init).reshape(1, 1)
    gsub = (diff_subln[lyr].astype(F32) * (1.0 - lambda_init))[:, None]

    x1 = _ffn(x.reshape(b * s, d), ffn1_norm[lyr][None, :],
              ffn1_w_gate[lyr].astype(BF16), ffn1_w_up[lyr].astype(BF16), ffn1_w_down[lyr].astype(BF16))
    x1 = x1.reshape(b, s, d)

    qa, ka, va, qb, kb, vb = _proj(x1, mix_norm[lyr][None, :], w_in_t, cos_t, sin_t, gq, gk)

    oa = _attn_a(lam, qa, ka, va, bias, gsub)
    ob = _attn_b(qb, kb, vb)

    return _tail(x1, oa, ob, w_out[lyr].astype(BF16), ffn2_norm[lyr][None, :],
                 ffn2_w_gate[lyr].astype(BF16), ffn2_w_up[lyr].astype(BF16), ffn2_w_down[lyr].astype(BF16),
                 final_norm[None, :])
```

```python
import math

import jax
import jax.numpy as jnp
from jax import lax
from jax.experimental import pallas as pl
from jax.experimental.pallas import tpu as pltpu

D_MODEL = 1024
HEAD_DIM = 64
D_MIX_A = 512
D_MIX_B = 512
N_HEADS_A = 8
DIFF_QK_DIM = 32
N_HEADS_B = 8
GQA_GROUP = 4
N_KV_B = 2
D_FF = 2816
GRID_W = 64
ROPE_THETA = 10000.0
ROPE_AXIS_DIM = 32
N_BUCKETS = 32
MAX_DISTANCE = 128
EPS = 1e-6
D_IN_PROJ = 2304
LOG2E = math.log2(math.e)

VMEM_LIMIT_BYTES = 56 * 1024 * 1024
BF16_SUBLANE_TILE = 16

FFN_TOKEN_TILE = 1024
FFN_CHUNKS = ((0, 1024), (1024, 2048), (2048, 2816))
ATTN_Q_TILE = 512
ATTN_K_TILE = 256
ATTN_A_TILES_PER_ITER = 2
ATTN_A_HEADS_PER_STEP = 4
V_ROWS = HEAD_DIM + BF16_SUBLANE_TILE

BF16 = jnp.bfloat16
F32 = jnp.float32


def _dot(a, b):
    return jnp.dot(a, b, preferred_element_type=F32)


def _rms_rows(x, g):
    ms = jnp.mean(x * x, axis=-1, keepdims=True)
    return x * lax.rsqrt(ms + EPS) * g


def _swiglu_acc(h, wg_ref, wu_ref, wd_ref):
    y = None
    for s, e in FFN_CHUNKS:
        g = _dot(h, wg_ref[:, s:e])
        u = _dot(h, wu_ref[:, s:e])
        a = (g / (1.0 + jnp.exp(-g)) * u).astype(BF16)
        part = _dot(a, wd_ref[s:e, :])
        y = part if y is None else y + part
    return y


def _resident(shape):
    return pl.BlockSpec(shape, lambda *_: (0,) * len(shape), pipeline_mode=pl.Buffered(1))


def _params(n_grid_axes):
    return pltpu.CompilerParams(dimension_semantics=("arbitrary",) * n_grid_axes,
                                vmem_limit_bytes=VMEM_LIMIT_BYTES)


def _toeplitz_kernel(base_ref, o_ref):
    nt, tk, tq = o_ref.shape
    for d in range(nt):
        rows = jnp.broadcast_to(base_ref[d], (tk, tq + tk))
        skew = pltpu.roll(rows, 0, 1, stride=1, stride_axis=0)
        o_ref[d] = skew[:, tk:tk + tq]


def _bias_tiles(base, tq, tk):
    h, nt = base.shape[:2]
    return pl.pallas_call(
        _toeplitz_kernel,
        grid=(h,),
        in_specs=[pl.BlockSpec((None, nt, 1, tq + tk), lambda hi: (hi, 0, 0, 0))],
        out_specs=pl.BlockSpec((None, nt, tk, tq), lambda hi: (hi, 0, 0, 0)),
        out_shape=jax.ShapeDtypeStruct((h, nt, tk, tq), F32),
        compiler_params=_params(1),
        name="bias_tiles",
    )(base)


def _ffn_kernel(x_ref, g_ref, wg_ref, wu_ref, wd_ref, o_ref):
    x = x_ref[...]
    h = _rms_rows(x, g_ref[...]).astype(BF16)
    o_ref[...] = x + 0.5 * _swiglu_acc(h, wg_ref, wu_ref, wd_ref)


def _ffn(x2d, g, wg, wu, wd):
    n = x2d.shape[0]
    tm = FFN_TOKEN_TILE
    return pl.pallas_call(
        _ffn_kernel,
        grid=(n // tm,),
        in_specs=[
            pl.BlockSpec((tm, D_MODEL), lambda i: (i, 0)),
            _resident((1, D_MODEL)),
            _resident((D_MODEL, D_FF)),
            _resident((D_MODEL, D_FF)),
            _resident((D_FF, D_MODEL)),
        ],
        out_specs=pl.BlockSpec((tm, D_MODEL), lambda i: (i, 0)),
        out_shape=jax.ShapeDtypeStruct((n, D_MODEL), F32),
        compiler_params=_params(1),
        name="ffn1",
    )(x2d, g, wg, wu, wd)


def _proj_kernel(x_ref, g_ref, wt_ref, cos_ref, sin_ref, gq_ref, gk_ref,
                 qa_ref, ka_ref, va_ref, qb_ref, kb_ref, vb_ref):
    tk = ATTN_K_TILE
    n_chunks = x_ref.shape[0] // tk
    h = _rms_rows(x_ref[...], g_ref[...]).astype(BF16)
    cos = cos_ref[...]
    sin = sin_ref[...]
    ones = jnp.ones((V_ROWS - HEAD_DIM, tk), BF16)

    def section(r0, r1):
        return lax.dot_general(wt_ref[r0:r1, :], h, (((1,), (1,)), ((), ())),
                               preferred_element_type=F32)

    def norm_rope(t, gain, scale):
        ms = jnp.mean(t * t, axis=0, keepdims=True)
        t = t * lax.rsqrt(ms + EPS) * gain
        te, to = t[:32], t[32:]
        re = te * cos - to * sin
        ro = te * sin + to * cos
        return jnp.concatenate([re, ro], axis=0) * scale

    def put_keys(dst_ref, head, kt):
        for j in range(n_chunks):
            dst_ref[head, j] = kt[:, j * tk:(j + 1) * tk].T.astype(BF16)

    def put_values(dst_ref, head, vt):
        for j in range(n_chunks):
            dst_ref[head, j, 0:HEAD_DIM, :] = vt[:, j * tk:(j + 1) * tk].astype(BF16)
            dst_ref[head, j, HEAD_DIM:V_ROWS, :] = ones

    def head_rows(t, head):
        return t[head * HEAD_DIM:(head + 1) * HEAD_DIM]

    qa = section(0, 512) * (DIFF_QK_DIM ** -0.5 * LOG2E)
    for hh in range(N_HEADS_A):
        qa_ref[hh] = head_rows(qa, hh).astype(BF16)
    ka = section(512, 1024)
    for hh in range(N_HEADS_A):
        put_keys(ka_ref, hh, head_rows(ka, hh))
    va = section(1024, 1536)
    for hh in range(N_HEADS_A):
        put_values(va_ref, hh, head_rows(va, hh))

    qb = section(1536, 2048)
    for hh in range(N_HEADS_B):
        qb_ref[hh // GQA_GROUP, hh % GQA_GROUP] = norm_rope(
            head_rows(qb, hh), gq_ref[...], HEAD_DIM ** -0.5 * LOG2E).astype(BF16)
    kvb = section(2048, 2304)
    for hh in range(N_KV_B):
        put_keys(kb_ref, hh, norm_rope(head_rows(kvb, hh), gk_ref[...], 1.0))
        put_values(vb_ref, hh, head_rows(kvb, N_KV_B + hh))


def _proj(x1, g, w_in_t, cos_t, sin_t, gq, gk):
    b, s, _ = x1.shape
    tm, tk = ATTN_Q_TILE, ATTN_K_TILE
    nq, nc, cpt = s // tm, s // tk, tm // tk
    out_shape = (
        jax.ShapeDtypeStruct((b, N_HEADS_A, nq, HEAD_DIM, tm), BF16),
        jax.ShapeDtypeStruct((b, N_HEADS_A, nc, tk, HEAD_DIM), BF16),
        jax.ShapeDtypeStruct((b, N_HEADS_A, nc, V_ROWS, tk), BF16),
        jax.ShapeDtypeStruct((b, N_KV_B, nq, GQA_GROUP, HEAD_DIM, tm), BF16),
        jax.ShapeDtypeStruct((b, N_KV_B, nc, tk, HEAD_DIM), BF16),
        jax.ShapeDtypeStruct((b, N_KV_B, nc, V_ROWS, tk), BF16),
    )
    out_specs = (
        pl.BlockSpec((None, N_HEADS_A, None, HEAD_DIM, tm), lambda bi, si: (bi, 0, si, 0, 0)),
        pl.BlockSpec((None, N_HEADS_A, cpt, tk, HEAD_DIM), lambda bi, si: (bi, 0, si, 0, 0)),
        pl.BlockSpec((None, N_HEADS_A, cpt, V_ROWS, tk), lambda bi, si: (bi, 0, si, 0, 0)),
        pl.BlockSpec((None, N_KV_B, None, GQA_GROUP, HEAD_DIM, tm), lambda bi, si: (bi, 0, si, 0, 0, 0)),
        pl.BlockSpec((None, N_KV_B, cpt, tk, HEAD_DIM), lambda bi, si: (bi, 0, si, 0, 0)),
        pl.BlockSpec((None, N_KV_B, cpt, V_ROWS, tk), lambda bi, si: (bi, 0, si, 0, 0)),
    )
    return pl.pallas_call(
        _proj_kernel,
        grid=(b, nq),
        in_specs=[
            pl.BlockSpec((None, tm, D_MODEL), lambda bi, si: (bi, si, 0)),
            _resident((1, D_MODEL)),
            _resident((D_IN_PROJ, D_MODEL)),
            pl.BlockSpec((ROPE_AXIS_DIM, tm), lambda bi, si: (0, si)),
            pl.BlockSpec((ROPE_AXIS_DIM, tm), lambda bi, si: (0, si)),
            _resident((HEAD_DIM, 1)),
            _resident((HEAD_DIM, 1)),
        ],
        out_specs=out_specs,
        out_shape=out_shape,
        compiler_params=_params(2),
        name="in_proj",
    )(x1, g, w_in_t, cos_t, sin_t, gq, gk)


def _fold8(x, op):
    rows, cols = x.shape
    return op(x.reshape(rows // 8, 8, cols), axis=0)


class _NoBias:
    qk_ahead = 1

    def chunk(self, j, first):
        return j

    def tile(self, j, first, head):
        return None

    def shift(self, j, first, head):
        return None


class _RelativeBias:
    qk_ahead = 0

    def __init__(self, bias_ref, nc, ratio):
        self.bias_ref, self.nc, self.ratio = bias_ref, nc, ratio
        self.nt = bias_ref.shape[1]

    def chunk(self, j, first):
        c = first + (j - 1)
        return jnp.where(c < 0, c + self.nc, jnp.where(c >= self.nc, c - self.nc, c))

    def tile(self, j, first, head):
        if j - 1 > self.ratio:
            return None
        c = first + (j - 1)
        d = jnp.where(c < 0, self.nt - 1, jnp.where(c >= self.nc, 0, j + 1))
        return self.bias_ref[head, d]

    def shift(self, j, first, head):
        if j - 1 <= self.ratio:
            return None
        row_before = self.bias_ref[head, 0, 0:1, :]
        row_after = self.bias_ref[head, self.nt - 1, 0:1, :]
        return jnp.where(first + (j - 1) >= self.nc, row_before, row_after)


def _stream_step(k_ref, vt_ref, s_ref, slot0, plan, prev, new):
    nc = k_ref.shape[1]
    m8 = None
    acc = None
    ahead = []
    if new is not None:
        weights, first, head = new
        for j in range(min(plan.qk_ahead, nc)):
            ahead.append(_dot(k_ref[head, plan.chunk(j, first)], weights))
    for j in range(nc):
        if prev is not None:
            prev_max, prev_first, prev_head = prev
            s_prev = s_ref[slot0 + j]
        if new is not None:
            if j + plan.qk_ahead < nc:
                ahead.append(_dot(k_ref[head, plan.chunk(j + plan.qk_ahead, first)], weights))
            s = ahead.pop(0)
            tile = plan.tile(j, first, head)
            if tile is not None:
                s = s + tile
            s_ref[slot0 + j] = s
            cm = _fold8(s, jnp.max)
            shift = plan.shift(j, first, head)
            if shift is not None:
                cm = cm + shift
            m8 = cm if m8 is None else jnp.maximum(m8, cm)
        if prev is not None:
            shift = plan.shift(j, prev_first, prev_head)
            p = jnp.exp2(s_prev - (prev_max if shift is None else prev_max - shift))
            pv = _dot(vt_ref[prev_head, plan.chunk(j, prev_first)], p.astype(BF16))
            acc = pv if acc is None else acc + pv
    new_max = None if m8 is None else jnp.max(m8, axis=0, keepdims=True)
    return acc, new_max


def _normalized(acc):
    return acc[:HEAD_DIM] / acc[HEAD_DIM:HEAD_DIM + 1]


def _split_index(it, inner):
    assert inner & (inner - 1) == 0
    return lax.shift_right_logical(it, inner.bit_length() - 1), it & (inner - 1)


def _attn_a_kernel(lam_ref, zero_ref, q_ref, k_ref, vt_ref, bias_ref, gsub_ref, o_ref, s_ref):
    heads, nq, _, tq = q_ref.shape
    nc = k_ref.shape[1]
    ratio = tq // ATTN_K_TILE
    slot0 = zero_ref[0]
    plan = _RelativeBias(bias_ref, nc, ratio)
    row = lax.broadcasted_iota(jnp.int32, (HEAD_DIM, tq), 0)
    n_tiles = heads * nq

    def weights(head, t, part):
        qt = q_ref[head, t]
        keep = (row < DIFF_QK_DIM) if part == 0 else (row >= DIFF_QK_DIM)
        return jnp.where(keep, qt, jnp.zeros_like(qt))

    def step(prev, new):
        return _stream_step(k_ref, vt_ref, s_ref, slot0, plan, prev, new)

    _, m0 = step(None, (weights(0, 0, 0), 0, 0))

    def body(i, m):
        for u in range(ATTN_A_TILES_PER_ITER):
            it = i * ATTN_A_TILES_PER_ITER + u
            head, t = _split_index(it, nq)
            head_n, t_n = _split_index(jnp.minimum(it + 1, n_tiles - 1), nq)
            acc1, m = step((m, t * ratio, head), (weights(head, t, 1), t * ratio, head))
            acc2, m = step((m, t * ratio, head), (weights(head_n, t_n, 0), t_n * ratio, head_n))
            oa = _normalized(acc1) - lam_ref[0, 0] * _normalized(acc2)
            ms = jnp.mean(oa * oa, axis=0, keepdims=True)
            o_ref[head, t] = (oa * lax.rsqrt(ms + EPS) * gsub_ref[...]).astype(BF16)
        return m

    lax.fori_loop(0, n_tiles // ATTN_A_TILES_PER_ITER, body, m0)


def _attn_b_kernel(zero_ref, q_ref, k_ref, vt_ref, o_ref, s_ref):
    heads, nq = q_ref.shape[:2]
    slot0 = zero_ref[0]
    plan = _NoBias()
    n_tiles = heads * nq
    views = (s_ref.at[0], s_ref.at[1])

    def step(buf, prev, new):
        return _stream_step(k_ref, vt_ref, views[buf], slot0, plan, prev, new)

    _, m0 = step(0, None, (q_ref[0, 0, 0], 0, 0))
    _, m1 = step(1, None, (q_ref[0, 0, 1], 0, 0))

    def body(it, carry):
        m_cur, m_nxt = carry
        head, t = _split_index(it, nq)
        head_n, t_n = _split_index(jnp.minimum(it + 1, n_tiles - 1), nq)
        for g in range(GQA_GROUP):
            if g + 2 < GQA_GROUP:
                new = (q_ref[head, t, g + 2], 0, head)
            else:
                new = (q_ref[head_n, t_n, g + 2 - GQA_GROUP], 0, head_n)
            acc, m_new = step(g % 2, (m_cur, 0, head), new)
            o_ref[head, t, g] = _normalized(acc).astype(BF16)
            m_cur, m_nxt = m_nxt, m_new
        return m_cur, m_nxt

    lax.fori_loop(0, n_tiles, body, (m0, m1))


def _attn_a(lam, q, k, vt, bias, gsub):
    b, h, nq, _, tq = q.shape
    nc, tk = k.shape[2], k.shape[3]
    nt = bias.shape[1]
    hps = ATTN_A_HEADS_PER_STEP
    return pl.pallas_call(
        _attn_a_kernel,
        grid=(b, h // hps),
        in_specs=[
            pl.BlockSpec(memory_space=pltpu.SMEM),
            pl.BlockSpec(memory_space=pltpu.SMEM),
            pl.BlockSpec((None, hps, nq, HEAD_DIM, tq), lambda bi, hi: (bi, hi, 0, 0, 0)),
            pl.BlockSpec((None, hps, nc, tk, HEAD_DIM), lambda bi, hi: (bi, hi, 0, 0, 0)),
            pl.BlockSpec((None, hps, nc, V_ROWS, tk), lambda bi, hi: (bi, hi, 0, 0, 0)),
            pl.BlockSpec((hps, nt, tk, tq), lambda bi, hi: (hi, 0, 0, 0)),
            pl.BlockSpec((HEAD_DIM, 1), lambda bi, hi: (0, 0)),
        ],
        out_specs=pl.BlockSpec((None, hps, nq, HEAD_DIM, tq), lambda bi, hi: (bi, hi, 0, 0, 0)),
        out_shape=jax.ShapeDtypeStruct((b, h, nq, HEAD_DIM, tq), BF16),
        scratch_shapes=[pltpu.VMEM((nc, tk, tq), F32)],
        compiler_params=_params(2),
        name="attn_diff",
    )(lam, jnp.zeros((1,), jnp.int32), q, k, vt, bias, gsub)


def _attn_b(q, k, vt):
    b, nkv, nq, grp, _, tq = q.shape
    nc, tk = k.shape[2], k.shape[3]
    return pl.pallas_call(
        _attn_b_kernel,
        grid=(b,),
        in_specs=[
            pl.BlockSpec(memory_space=pltpu.SMEM),
            pl.BlockSpec((None, nkv, nq, grp, HEAD_DIM, tq), lambda bi: (bi, 0, 0, 0, 0, 0)),
            pl.BlockSpec((None, nkv, nc, tk, HEAD_DIM), lambda bi: (bi, 0, 0, 0, 0)),
            pl.BlockSpec((None, nkv, nc, V_ROWS, tk), lambda bi: (bi, 0, 0, 0, 0)),
        ],
        out_specs=pl.BlockSpec((None, nkv, nq, grp, HEAD_DIM, tq), lambda bi: (bi, 0, 0, 0, 0, 0)),
        out_shape=jax.ShapeDtypeStruct((b, nkv, nq, grp, HEAD_DIM, tq), BF16),
        scratch_shapes=[pltpu.VMEM((2, nc, tk, tq), F32)],
        compiler_params=_params(1),
        name="attn_gqa",
    )(jnp.zeros((1,), jnp.int32), q, k, vt)


def _tail_kernel(x_ref, oa_ref, ob_ref, wo_ref, g2_ref, wg_ref, wu_ref, wd_ref, gf_ref, o_ref):
    tn = (((0,), (0,)), ((), ()))
    n_tiles, tq = oa_ref.shape[1], oa_ref.shape[-1]
    mixed = []
    for i in range(n_tiles):
        oa = oa_ref[:, i].reshape(D_MIX_A, tq)
        ob = ob_ref[:, i].reshape(D_MIX_B, tq)
        mixed.append(lax.dot_general(oa, wo_ref[0:D_MIX_A, :], tn, preferred_element_type=F32)
                     + lax.dot_general(ob, wo_ref[D_MIX_A:, :], tn, preferred_element_type=F32))
    x2 = x_ref[...] + (mixed[0] if n_tiles == 1 else jnp.concatenate(mixed, axis=0))
    h = _rms_rows(x2, g2_ref[...]).astype(BF16)
    y = x2 + 0.5 * _swiglu_acc(h, wg_ref, wu_ref, wd_ref)
    o_ref[...] = _rms_rows(y, gf_ref[...])


def _tail(x1, oa, ob, wo, g2, wg, wu, wd, gf):
    b, s, _ = x1.shape
    tm, tq = FFN_TOKEN_TILE, ATTN_Q_TILE
    tpb = tm // tq
    return pl.pallas_call(
        _tail_kernel,
        grid=(b, s // tm),
        in_specs=[
            pl.BlockSpec((None, tm, D_MODEL), lambda bi, si: (bi, si, 0)),
            pl.BlockSpec((None, N_HEADS_A, tpb, HEAD_DIM, tq), lambda bi, si: (bi, 0, si, 0, 0)),
            pl.BlockSpec((None, N_KV_B, tpb, GQA_GROUP, HEAD_DIM, tq), lambda bi, si: (bi, 0, si, 0, 0, 0)),
            _resident((D_MODEL, D_MODEL)),
            _resident((1, D_MODEL)),
            _resident((D_MODEL, D_FF)),
            _resident((D_MODEL, D_FF)),
            _resident((D_FF, D_MODEL)),
            _resident((1, D_MODEL)),
        ],
        out_specs=pl.BlockSpec((None, tm, D_MODEL), lambda bi, si: (bi, si, 0)),
        out_shape=jax.ShapeDtypeStruct((b, s, D_MODEL), F32),
        compiler_params=_params(2),
        name="out_proj_ffn2",
    )(x1, oa, ob, wo, g2, wg, wu, wd, gf)


def _t5_bucket(rel):
    nb = N_BUCKETS // 2
    max_exact = nb // 2
    ret = jnp.where(rel > 0, nb, 0)
    n = jnp.abs(rel)
    nf = jnp.maximum(n, 1).astype(jnp.float32)
    large = max_exact + (jnp.log(nf / max_exact) / math.log(MAX_DISTANCE / max_exact)
                         * (nb - max_exact)).astype(jnp.int32)
    large = jnp.minimum(large, nb - 1)
    return ret + jnp.where(n < max_exact, n, large)


def _bias_base_rows(rel_bias, tq, tk):
    nt = tq // tk + 4
    delta = jnp.arange(-2, nt - 2, dtype=jnp.int32)[:, None] * tk
    y = jnp.arange(tq + tk, dtype=jnp.int32)[None, :]
    table = (rel_bias.astype(F32) * LOG2E).T
    bucket = _t5_bucket(delta + tk - y)
    hit = bucket[None, None] == jnp.arange(N_BUCKETS, dtype=jnp.int32)[None, :, None, None]
    base = jnp.sum(jnp.where(hit, table[:, :, None, None], 0.0), axis=1)
    return base[:, :, None, :]


def _rope_tables_t(s):
    rows = s // GRID_W
    row = jnp.repeat(jnp.arange(rows, dtype=jnp.int32), GRID_W).astype(F32)
    col = jnp.tile(jnp.arange(GRID_W, dtype=jnp.int32), rows).astype(F32)
    inv = ROPE_THETA ** (-jnp.arange(0, ROPE_AXIS_DIM, 2, dtype=F32) / ROPE_AXIS_DIM)
    ang = jnp.concatenate([row[:, None] * inv[None], col[:, None] * inv[None]], axis=-1)
    return jnp.cos(ang).T, jnp.sin(ang).T


def kernel(x, ffn1_norm, ffn1_w_gate, ffn1_w_up, ffn1_w_down, mix_norm, w_in, lambda_q1, lambda_k1, lambda_q2, lambda_k2, diff_subln, q_norm, k_norm, rel_bias, w_out, ffn2_norm, ffn2_w_gate, ffn2_w_up, ffn2_w_down, final_norm):
    b, s, d = x.shape
    tq, tk = ATTN_Q_TILE, ATTN_K_TILE
    lyr = 0
    lambda_init = 0.8 - 0.6 * math.exp(-0.3 * lyr)

    def deinterleave(t, n_heads):
        lead = t.shape[:-1]
        t = t.reshape(lead + (n_heads, HEAD_DIM // 2, 2))
        return jnp.swapaxes(t, -1, -2).reshape(lead + (n_heads * HEAD_DIM,))

    w = w_in[lyr]
    w_in_t = jnp.concatenate(
        [w[:, :1536], deinterleave(w[:, 1536:2048], N_HEADS_B), deinterleave(w[:, 2048:2176], N_KV_B),
         w[:, 2176:]], axis=1).T.astype(BF16)
    gq = deinterleave(q_norm[lyr].astype(F32), 1)[:, None]
    gk = deinterleave(k_norm[lyr].astype(F32), 1)[:, None]
    cos_t, sin_t = _rope_tables_t(s)
    bias = _bias_tiles(_bias_base_rows(rel_bias, tq, tk), tq, tk)
    lam = (jnp.exp(jnp.sum(lambda_q1[lyr].astype(F32) * lambda_k1[lyr].astype(F32)))
           - jnp.exp(jnp.sum(lambda_q2[lyr].astype(F32) * lambda_k2[lyr].astype(F32)))
           + lambda_init).reshape(1, 1)
    gsub = (diff_subln[lyr].astype(F32) * (1.0 - lambda_init))[:, None]

    x1 = _ffn(x.reshape(b * s, d), ffn1_norm[lyr][None, :],
              ffn1_w_gate[lyr].astype(BF16), ffn1_w_up[lyr].astype(BF16), ffn1_w_down[lyr].astype(BF16))
    x1 = x1.reshape(b, s, d)

    qa, ka, va, qb, kb, vb = _proj(x1, mix_norm[lyr][None, :], w_in_t, cos_t, sin_t, gq, gk)

    oa = _attn_a(lam, qa, ka, va, bias, gsub)
    ob = _attn_b(qb, kb, vb)

    return _tail(x1, oa, ob, w_out[lyr].astype(BF16), ffn2_norm[lyr][None, :],
                 ffn2_w_gate[lyr].astype(BF16), ffn2_w_up[lyr].astype(BF16), ffn2_w_down[lyr].astype(BF16),
                 final_norm[None, :])
```
